```python
import jax, jax.numpy as jnp
from jax import lax
import numpy as np

D_MODEL = 1024
BATCH = 2
SEQ = 8192
DEPTH = 1
DEC_BATCH = 32
DEC_SEQ = 8
PAST_LEN = 8192
PAGE_SIZE = 128

HEAD_DIM = 64
N_ATTN_HEADS = 8
ATTN_WIDTH = N_ATTN_HEADS * HEAD_DIM
N_IDX_HEADS = 8
IDX_DIM = 64
MAX_TOPK = 256
N_GMLP_GROUPS = 8
GMLP_WIDTH = N_GMLP_GROUPS * HEAD_DIM
CHUNK = 128
MIX_WIDTH = ATTN_WIDTH + GMLP_WIDTH
IN_WIDTH = 3 * ATTN_WIDTH + N_IDX_HEADS * IDX_DIM + IDX_DIM + N_IDX_HEADS + 2 * GMLP_WIDTH
ROPE_THETA = 10000.0
N_PEER_HEADS = 8
N_KEYS = 128
N_EXPERTS = N_KEYS * N_KEYS
PEER_QDIM = 128
PEER_HALF = PEER_QDIM // 2
PEER_TOPK = 16
PEER_BLOCK = 128
QUERY_BLOCK = 128
EPS = 1e-6

kernel_name = "hymba_dsa_gmlp_peer_step"


def rmsnorm(x, g):
    xf = x.astype(jnp.float32)
    y = xf * lax.rsqrt(jnp.mean(xf * xf, axis=-1, keepdims=True) + EPS)
    return (y * g.astype(jnp.float32)).astype(x.dtype)


def rope(x, pos):
    half = x.shape[-1] // 2
    inv_freq = ROPE_THETA ** (-jnp.arange(half, dtype=jnp.float32) / half)
    ang = pos.astype(jnp.float32)[:, None] * inv_freq[None, :]
    cos = jnp.cos(ang)[None, :, None, :]
    sin = jnp.sin(ang)[None, :, None, :]
    xf = x.astype(jnp.float32)
    x1, x2 = xf[..., :half], xf[..., half:]
    return jnp.concatenate([x1 * cos - x2 * sin, x2 * cos + x1 * sin], axis=-1).astype(x.dtype)


def mixer_projections(h, w_in_l, gv_g_l, pos):
    B, T, _ = h.shape
    sizes = [ATTN_WIDTH, ATTN_WIDTH, ATTN_WIDTH, N_IDX_HEADS * IDX_DIM, IDX_DIM, N_IDX_HEADS, GMLP_WIDTH]
    points = [int(p) for p in np.cumsum(sizes)]
    proj = h @ w_in_l
    q, k, v, qi, ki, wi, u, gv = jnp.split(proj, points, axis=-1)
    q = rope(q.reshape(B, T, N_ATTN_HEADS, HEAD_DIM), pos)
    k = rope(k.reshape(B, T, N_ATTN_HEADS, HEAD_DIM), pos)
    v = v.reshape(B, T, N_ATTN_HEADS, HEAD_DIM)
    qi = rope(qi.reshape(B, T, N_IDX_HEADS, IDX_DIM), pos)
    ki = rope(ki.reshape(B, T, 1, IDX_DIM), pos)[:, :, 0, :]
    u = jax.nn.gelu(u).reshape(B, T, N_GMLP_GROUPS, HEAD_DIM)
    gv = jax.nn.gelu(gv).reshape(B, T, N_GMLP_GROUPS, HEAD_DIM)
    gv = rmsnorm(gv, gv_g_l.reshape(N_GMLP_GROUPS, HEAD_DIM))
    return q, k, v, qi, ki, wi, u, gv


def indexer_scores(qi, wi, ki):
    s = jnp.einsum('bthd,bsd->bths', qi.astype(jnp.float32), ki.astype(jnp.float32))
    s = jax.nn.relu(s * (IDX_DIM ** -0.5))
    w = wi.astype(jnp.float32) * (N_IDX_HEADS ** -0.5)
    return jnp.einsum('bths,bth->bts', s, w)


def sparse_attend(q, ks, vs, valid):
    logits = jnp.einsum('bthd,btkhd->bthk', q, ks).astype(jnp.float32) * (HEAD_DIM ** -0.5)
    logits = jnp.where(valid[:, :, None, :], logits, -jnp.inf)
    p = jax.nn.softmax(logits, axis=-1)
    return jnp.einsum('bthk,btkhd->bthd', p.astype(vs.dtype), vs)


def gather_rows(arr, idx):
    return jax.vmap(lambda a, i: a[i])(arr, idx)


def dsa_prompt(q, k, v, qi, ki, wi):
    B, T = q.shape[:2]
    n_sel = min(MAX_TOPK, T // 4)
    n_blk = T // QUERY_BLOCK
    key_pos = jnp.arange(T)

    def block(i):
        start = i * QUERY_BLOCK
        qb = lax.dynamic_slice_in_dim(q, start, QUERY_BLOCK, axis=1)
        qib = lax.dynamic_slice_in_dim(qi, start, QUERY_BLOCK, axis=1)
        wib = lax.dynamic_slice_in_dim(wi, start, QUERY_BLOCK, axis=1)
        qpos = start + jnp.arange(QUERY_BLOCK)
        score = indexer_scores(qib, wib, ki)
        causal = key_pos[None, :] <= qpos[:, None]
        score = jnp.where(causal[None], score, -jnp.inf)
        _, sel = lax.top_k(score, n_sel)
        valid = sel <= qpos[None, :, None]
        return sparse_attend(qb, gather_rows(k, sel), gather_rows(v, sel), valid)

    out = lax.map(block, jnp.arange(n_blk))
    return out.transpose(1, 0, 2, 3, 4).reshape(B, T, N_ATTN_HEADS, HEAD_DIM)


def dsa_sample(q, k_new, v_new, qi, ki_new, wi, cache_k_l, cache_v_l, cache_ki_l, page_table):
    Bd, Tn = q.shape[:2]
    past = page_table.shape[1] * PAGE_SIZE
    L = past + Tn
    n_sel = min(MAX_TOPK, L // 4)
    ki_past = cache_ki_l[page_table].reshape(Bd, past, IDX_DIM)
    ki_all = jnp.concatenate([ki_past, ki_new.astype(ki_past.dtype)], axis=1)
    qpos = past + jnp.arange(Tn)
    key_pos = jnp.arange(L)
    score = indexer_scores(qi, wi, ki_all)
    score = jnp.where((key_pos[None, :] <= qpos[:, None])[None], score, -jnp.inf)
    _, sel = lax.top_k(score, n_sel)
    valid = sel <= qpos[None, :, None]
    from_past = sel < past
    sel_p = jnp.minimum(sel, past - 1)
    phys = jax.vmap(lambda pt, s: pt[s])(page_table, sel_p // PAGE_SIZE)
    off = sel_p % PAGE_SIZE
    sel_n = jnp.clip(sel - past, 0, Tn - 1)
    fp = from_past[..., None, None]
    ks = jnp.where(fp, cache_k_l[phys, off], gather_rows(k_new, sel_n).astype(cache_k_l.dtype))
    vs = jnp.where(fp, cache_v_l[phys, off], gather_rows(v_new, sel_n).astype(cache_v_l.dtype))
    return sparse_attend(q, ks.astype(q.dtype), vs.astype(q.dtype), valid)


def chunk_gmlp(u, gv, w_sp_l, b_sp_l):
    B, T = u.shape[:2]
    n_chunk = -(-T // CHUNK)
    pad = n_chunk * CHUNK - T
    gvp = jnp.pad(gv, ((0, 0), (0, pad), (0, 0), (0, 0))).reshape(B, n_chunk, CHUNK, N_GMLP_GROUPS, HEAD_DIM)
    mask = jnp.tril(jnp.ones((CHUNK, CHUNK), dtype=bool))
    w = jnp.where(mask[None], w_sp_l, 0).astype(gv.dtype)
    mixed = jnp.einsum('gts,bnsgd->bntgd', w, gvp) + b_sp_l.T[None, None, :, :, None].astype(gv.dtype)
    mixed = mixed.reshape(B, n_chunk * CHUNK, N_GMLP_GROUPS, HEAD_DIM)[:, :T]
    return u * mixed


def peer(x, w_pq_l, sub_keys_l, expert_u_l, expert_v_l):
    N = x.shape[0]
    n_blk = -(-N // PEER_BLOCK)
    xp = jnp.pad(x, ((0, n_blk * PEER_BLOCK - N), (0, 0))).reshape(n_blk, PEER_BLOCK, D_MODEL)

    def block(xb):
        q = (xb @ w_pq_l).reshape(PEER_BLOCK, N_PEER_HEADS, 2, PEER_HALF)
        s = jnp.einsum('bhcd,hckd->bhck', q, sub_keys_l).astype(jnp.float32)
        s_top, i_top = lax.top_k(s, PEER_TOPK)
        cand = s_top[:, :, 0, :, None] + s_top[:, :, 1, None, :]
        cand_idx = i_top[:, :, 0, :, None] * N_KEYS + i_top[:, :, 1, None, :]
        cand = cand.reshape(PEER_BLOCK, N_PEER_HEADS, PEER_TOPK * PEER_TOPK)
        cand_idx = cand_idx.reshape(PEER_BLOCK, N_PEER_HEADS, PEER_TOPK * PEER_TOPK)
        best, pos = lax.top_k(cand, PEER_TOPK)
        e = jnp.take_along_axis(cand_idx, pos, axis=-1)
        g = jax.nn.softmax(best, axis=-1)
        ue = expert_u_l[e]
        ve = expert_v_l[e]
        act = jax.nn.gelu(jnp.einsum('bhkd,bd->bhk', ue, xb).astype(jnp.float32))
        return jnp.einsum('bhk,bhkd->bd', (g * act).astype(xb.dtype), ve)

    out = lax.map(block, xp).reshape(n_blk * PEER_BLOCK, D_MODEL)
    return out[:N]


def setup_inputs(seed: int = 0) -> dict:
    key = jax.random.key(seed)
    ks = jax.random.split(key, 20)
    n_pages = PAST_LEN // PAGE_SIZE
    n_used = DEC_BATCH * n_pages
    n_pool = (n_used * 5) // 4
    f32 = jnp.float32
    nrm = lambda k, shape, s: jax.random.normal(k, shape, f32) * s
    page_table = jax.random.permutation(ks[5], n_pool)[:n_used].reshape(DEC_BATCH, n_pages).astype(jnp.int32)
    return {
        "x_prompt": nrm(ks[0], (BATCH, SEQ, D_MODEL), 1.0),
        "x_sample": nrm(ks[1], (DEC_BATCH, DEC_SEQ, D_MODEL), 1.0),
        "cache_k": nrm(ks[2], (DEPTH, n_pool, PAGE_SIZE, N_ATTN_HEADS, HEAD_DIM), 1.0),
        "cache_v": nrm(ks[3], (DEPTH, n_pool, PAGE_SIZE, N_ATTN_HEADS, HEAD_DIM), 1.0),
        "cache_kidx": nrm(ks[4], (DEPTH, n_pool, PAGE_SIZE, IDX_DIM), 1.0),
        "page_table": page_table,
        "norm_mix_g": 1.0 + nrm(ks[6], (DEPTH, D_MODEL), 0.02),
        "w_in": nrm(ks[7], (DEPTH, D_MODEL, IN_WIDTH), D_MODEL ** -0.5),
        "gv_norm_g": 1.0 + nrm(ks[8], (DEPTH, GMLP_WIDTH), 0.02),
        "w_sp": nrm(ks[9], (DEPTH, N_GMLP_GROUPS, CHUNK, CHUNK), CHUNK ** -0.5),
        "b_sp": 1.0 + nrm(ks[10], (DEPTH, N_GMLP_GROUPS, CHUNK), 0.1),
        "w_out": nrm(ks[11], (DEPTH, MIX_WIDTH, D_MODEL), MIX_WIDTH ** -0.5),
        "norm_ffn_g": 1.0 + nrm(ks[12], (DEPTH, D_MODEL), 0.02),
        "w_pq": nrm(ks[13], (DEPTH, D_MODEL, N_PEER_HEADS * PEER_QDIM), D_MODEL ** -0.5),
        "peer_sub_keys": nrm(ks[14], (DEPTH, N_PEER_HEADS, 2, N_KEYS, PEER_HALF), PEER_HALF ** -0.5),
        "expert_u": nrm(ks[15], (DEPTH, N_EXPERTS, D_MODEL), D_MODEL ** -0.5),
        "expert_v": nrm(ks[16], (DEPTH, N_EXPERTS, D_MODEL), N_PEER_HEADS ** -0.5),
        "norm_final_g": 1.0 + nrm(ks[17], (D_MODEL,), 0.02),
    }


def reference(x_prompt, x_sample, cache_k, cache_v, cache_kidx, page_table, norm_mix_g, w_in,
              gv_norm_g, w_sp, b_sp, w_out, norm_ffn_g, w_pq, peer_sub_keys, expert_u, expert_v,
              norm_final_g):
    B, T = x_prompt.shape[:2]
    Bd, Tn = x_sample.shape[:2]
    past = page_table.shape[1] * PAGE_SIZE
    pos_p = jnp.arange(T, dtype=jnp.int32)
    pos_s = past + jnp.arange(Tn, dtype=jnp.int32)
    xp, xs = x_prompt, x_sample
    k_p, v_p, ki_p, k_s, v_s, ki_s, gv_s = [], [], [], [], [], [], []
    for l in range(DEPTH):
        h = rmsnorm(xp, norm_mix_g[l])
        q, k, v, qi, ki, wi, u, gv = mixer_projections(h, w_in[l], gv_norm_g[l], pos_p)
        a = dsa_prompt(q, k, v, qi, ki, wi).reshape(B, T, ATTN_WIDTH)
        g = chunk_gmlp(u, gv, w_sp[l], b_sp[l]).reshape(B, T, GMLP_WIDTH)
        xp = xp + jnp.concatenate([a, g], axis=-1) @ w_out[l]
        k_p.append(k); v_p.append(v); ki_p.append(ki)
        h = rmsnorm(xs, norm_mix_g[l])
        q, k, v, qi, ki, wi, u, gv = mixer_projections(h, w_in[l], gv_norm_g[l], pos_s)
        a = dsa_sample(q, k, v, qi, ki, wi, cache_k[l], cache_v[l], cache_kidx[l], page_table).reshape(Bd, Tn, ATTN_WIDTH)
        g = chunk_gmlp(u, gv, w_sp[l], b_sp[l]).reshape(Bd, Tn, GMLP_WIDTH)
        xs = xs + jnp.concatenate([a, g], axis=-1) @ w_out[l]
        k_s.append(k); v_s.append(v); ki_s.append(ki); gv_s.append(gv)
        hp = rmsnorm(xp, norm_ffn_g[l]).reshape(B * T, D_MODEL)
        xp = xp + peer(hp, w_pq[l], peer_sub_keys[l], expert_u[l], expert_v[l]).reshape(B, T, D_MODEL)
        hs = rmsnorm(xs, norm_ffn_g[l]).reshape(Bd * Tn, D_MODEL)
        xs = xs + peer(hs, w_pq[l], peer_sub_keys[l], expert_u[l], expert_v[l]).reshape(Bd, Tn, D_MODEL)
    y_prompt = rmsnorm(xp, norm_final_g)
    y_sample = rmsnorm(xs, norm_final_g)
    new_k_prompt = jnp.stack(k_p)
    new_v_prompt = jnp.stack(v_p)
    new_kidx_prompt = jnp.stack(ki_p)
    new_k_sample = jnp.stack(k_s)
    new_v_sample = jnp.stack(v_s)
    new_kidx_sample = jnp.stack(ki_s)
    new_gmlp_v_sample = jnp.stack(gv_s)
    return (y_prompt, y_sample, new_k_prompt, new_v_prompt, new_kidx_prompt, new_k_sample, new_v_sample, new_kidx_sample, new_gmlp_v_sample)
```

```python
import functools

import jax
import jax.numpy as jnp
from jax import lax
from jax.experimental import pallas as pl
from jax.experimental.pallas import tpu as pltpu

F32 = jnp.float32
BF16 = jnp.bfloat16
I32 = jnp.int32

EPS = 1e-6
ROPE_THETA = 10000.0
HEAD_DIM = 64
ROPE_HALF = HEAD_DIM // 2
N_HEADS = 8
ATTN_WIDTH = N_HEADS * HEAD_DIM
N_IDX_HEADS = 8
IDX_DIM = 64
MAX_TOPK = 256
N_GROUPS = 8
GMLP_WIDTH = N_GROUPS * HEAD_DIM
CHUNK = 128
PAGE_SIZE = 128
N_PEER_HEADS = 8
N_KEYS = 128
PEER_HALF = 64
PEER_TOPK = 16
LANES = 128

INT_MIN = -(2 ** 31)
NEG_BIG = -1e30
IDX_SCALE = (IDX_DIM ** -0.5) * (N_IDX_HEADS ** -0.5)
ATTN_SCALE = HEAD_DIM ** -0.5

VMEM_LIMIT_BYTES = 56 * 1024 * 1024


def _params(*sem):
    return pltpu.CompilerParams(dimension_semantics=sem, vmem_limit_bytes=VMEM_LIMIT_BYTES)


def _gelu(x):
    cdf = 0.5 * (1.0 + jnp.tanh(0.7978845608028654 * (x + 0.044715 * (x * x * x))))
    return x * cdf


def _rms(x, g):
    return x * lax.rsqrt(jnp.mean(x * x, axis=-1, keepdims=True) + EPS) * g


def _split3(x):
    a = x.astype(BF16)
    r = x - a.astype(F32)
    b = r.astype(BF16)
    c = (r - b.astype(F32)).astype(BF16)
    return a, b, c


def _dot_f32_by_bf16(x, m):
    a, b, c = _split3(x)
    d = lambda p: jnp.dot(p, m, preferred_element_type=F32)
    return d(a) + d(b) + d(c)


def _sortable(x):
    bits = lax.bitcast_convert_type(x, I32)
    return jnp.where(bits < 0, bits ^ 0x7FFFFFFF, bits)


def _kth_largest_key(count_ge, k, shape):
    c0 = count_ge(jnp.zeros(shape, I32))
    t0 = jnp.where(c0 >= k, 0, INT_MIN).astype(I32)

    def body(i, t):
        cand = t + lax.shift_left(jnp.int32(1), 30 - i)
        return jnp.where(count_ge(cand) >= k, cand, t)

    return lax.fori_loop(0, 31, body, t0)


def _project_kernel(x_ref, pos_ref, invf_ref, g_ref, wm_ref, ws_ref, gvg_ref, gmean_ref,
                    q_ref, k_ref, v_ref, qi_ref, u_ref, gv_ref, small_ref):
    h = _rms(x_ref[...], g_ref[...]).astype(BF16)
    ang = pos_ref[...] * invf_ref[...]
    cos = jnp.cos(ang)
    sin = jnp.sin(ang)
    lane = lax.broadcasted_iota(I32, ang.shape, 1)
    first = (lane % HEAD_DIM) < ROPE_HALF
    sin_signed = jnp.where(first, -sin, sin)

    def rope(xc):
        partner = jnp.where(first, pltpu.roll(xc, LANES - ROPE_HALF, 1), pltpu.roll(xc, ROPE_HALF, 1))
        return xc * cos + partner * sin_signed

    def proj(c0):
        return jnp.dot(h, wm_ref[:, c0:c0 + ATTN_WIDTH], preferred_element_type=F32)

    for ref, c0 in ((q_ref, 0), (k_ref, ATTN_WIDTH), (qi_ref, 3 * ATTN_WIDTH)):
        p = proj(c0)
        for c in range(ATTN_WIDTH // LANES):
            ref[:, c * LANES:(c + 1) * LANES] = rope(p[:, c * LANES:(c + 1) * LANES])
    v_ref[...] = proj(2 * ATTN_WIDTH)
    u_ref[...] = _gelu(proj(4 * ATTN_WIDTH))
    gl = _gelu(proj(5 * ATTN_WIDTH))
    ms = _dot_f32_by_bf16(gl * gl, gmean_ref[...])
    gv_ref[...] = gl * lax.rsqrt(ms + EPS) * gvg_ref[...]
    sm = jnp.dot(h, ws_ref[...], preferred_element_type=F32)
    small_ref[...] = jnp.where(lane < IDX_DIM, rope(sm), sm)


def _project(x2d, pos, invf, g, wm, ws, gvg, gmean, tm):
    n, d = x2d.shape
    row = lambda i: (i, 0)
    fixed = lambda i: (0, 0)
    wide = jax.ShapeDtypeStruct((n, ATTN_WIDTH), F32)
    return pl.pallas_call(
        _project_kernel,
        grid=(n // tm,),
        in_specs=[
            pl.BlockSpec((tm, d), row),
            pl.BlockSpec((tm, 1), row),
            pl.BlockSpec((1, LANES), fixed),
            pl.BlockSpec((1, d), fixed),
            pl.BlockSpec(wm.shape, fixed),
            pl.BlockSpec(ws.shape, fixed),
            pl.BlockSpec((1, GMLP_WIDTH), fixed),
            pl.BlockSpec(gmean.shape, fixed),
        ],
        out_specs=[pl.BlockSpec((tm, ATTN_WIDTH), row)] * 6 + [pl.BlockSpec((tm, LANES), row)],
        out_shape=[wide] * 6 + [jax.ShapeDtypeStruct((n, LANES), F32)],
        compiler_params=_params("parallel"),
        name="project",
    )(x2d, pos, invf, g, wm, ws, gvg, gmean)


def _dsa_prompt_kernel(qi_ref, w_ref, ki_ref, qbd_ref, k_ref, vt_ref, out_ref, keys_ref,
                       *, tq, tk, n_sel):
    j = pl.program_id(1)
    nkb = (j + 1) * (tq // tk)
    q0 = j * tq
    qi = qi_ref[0, 0]
    w = w_ref[0, 0] * IDX_SCALE
    qpos = q0 + lax.broadcasted_iota(I32, (tk, tq), 1)
    krow = lax.broadcasted_iota(I32, (tk, tq), 0)

    def score_body(kb, carry):
        r0 = pl.multiple_of(kb * tk, tk)
        d = jnp.dot(ki_ref[0, pl.ds(r0, tk), :], qi, preferred_element_type=F32)
        s = jnp.zeros((tk, tq), F32)
        for h in range(N_IDX_HEADS):
            s = s + jnp.maximum(d[:, h * tq:(h + 1) * tq], 0.0) * w[:, h * tq:(h + 1) * tq]
        key = jnp.where(krow + r0 <= qpos, _sortable(s), INT_MIN)
        keys_ref[pl.ds(r0, tk), :] = key
        return carry

    lax.fori_loop(0, nkb, score_body, 0)

    def count_ge(cand):
        def body(kb, acc):
            blk = keys_ref[pl.ds(pl.multiple_of(kb * tk, tk), tk), :]
            return acc + jnp.sum(jnp.where(blk >= cand, 1.0, 0.0), axis=0, keepdims=True)
        return lax.fori_loop(0, nkb, body, jnp.zeros((1, tq), F32))

    thr = _kth_largest_key(count_ge, n_sel, (1, tq))
    thr = jnp.maximum(thr, INT_MIN + 1)
    thr2 = jnp.concatenate([thr, thr], axis=1)

    for p in range(N_HEADS // 2):
        qbd = qbd_ref[0, 0, p]

        def att_body(kb, carry, p=p, qbd=qbd):
            m, l, acc = carry
            r0 = pl.multiple_of(kb * tk, tk)
            kblk = k_ref[0, pl.ds(r0, tk), p * LANES:(p + 1) * LANES]
            lg = jnp.dot(kblk, qbd, preferred_element_type=F32) * ATTN_SCALE
            keyb = keys_ref[pl.ds(r0, tk), :]
            key2 = jnp.concatenate([keyb, keyb], axis=1)
            lm = jnp.where(key2 >= thr2, lg, NEG_BIG)
            m_new = jnp.maximum(m, jnp.max(lm, axis=0, keepdims=True))
            pe = jnp.exp(lm - m_new)
            alpha = jnp.exp(m - m_new)
            l_new = alpha * l + jnp.sum(pe, axis=0, keepdims=True)
            pb = pe.astype(BF16)
            vtb = vt_ref[0, kb, p * LANES:(p + 1) * LANES, :]
            pv0 = jnp.dot(vtb[:HEAD_DIM], pb[:, :tq], preferred_element_type=F32)
            pv1 = jnp.dot(vtb[HEAD_DIM:], pb[:, tq:], preferred_element_type=F32)
            acc_new = jnp.concatenate([alpha[:, :tq] * acc[:HEAD_DIM] + pv0,
                                       alpha[:, tq:] * acc[HEAD_DIM:] + pv1], axis=0)
            return m_new, l_new, acc_new

        init = (jnp.full((1, 2 * tq), NEG_BIG, F32), jnp.zeros((1, 2 * tq), F32),
                jnp.zeros((LANES, tq), F32))
        m, l, acc = lax.fori_loop(0, nkb, att_body, init)
        out_ref[0, p * LANES:(p + 1) * LANES, :] = jnp.concatenate(
            [acc[:HEAD_DIM] / l[:, :tq], acc[HEAD_DIM:] / l[:, tq:]], axis=0)


def _dsa_prompt(qi_t, w_t, ki, qbd, k, vt, tq, tk, n_sel):
    b, nq = qi_t.shape[:2]
    t = k.shape[1]
    tile = lambda bb, j: (bb, j, 0, 0)
    whole3 = lambda bb, j: (bb, 0, 0)
    return pl.pallas_call(
        functools.partial(_dsa_prompt_kernel, tq=tq, tk=tk, n_sel=n_sel),
        grid=(b, nq),
        in_specs=[
            pl.BlockSpec((1, 1, IDX_DIM, N_IDX_HEADS * tq), tile),
            pl.BlockSpec((1, 1, 1, N_IDX_HEADS * tq), tile),
            pl.BlockSpec((1, t, IDX_DIM), whole3, pipeline_mode=pl.Buffered(1)),
            pl.BlockSpec((1, 1, N_HEADS // 2, LANES, 2 * tq), lambda bb, j: (bb, j, 0, 0, 0)),
            pl.BlockSpec((1, t, ATTN_WIDTH), whole3, pipeline_mode=pl.Buffered(1)),
            pl.BlockSpec((1, t // tk, ATTN_WIDTH, tk), lambda bb, j: (bb, 0, 0, 0),
                         pipeline_mode=pl.Buffered(1)),
        ],
        out_specs=pl.BlockSpec((1, ATTN_WIDTH, tq), lambda bb, j: (bb, 0, j)),
        out_shape=jax.ShapeDtypeStruct((b, ATTN_WIDTH, t), F32),
        scratch_shapes=[pltpu.VMEM((t, tq), I32)],
        compiler_params=_params("parallel", "parallel"),
        name="dsa_prompt",
    )(qi_t, w_t, ki, qbd, k, vt)


PAGES_PER_STEP = 8


def _lane_group_allsum(x):
    for sh in (64, 32, 16, 8):
        x = x + pltpu.roll(x, sh, 1)
    return x


def _sample_score_kernel(pt_ref, qit_ref, w_ref, kinew_ref, *rest, n_sel, n_new):
    pages = rest[:PAGES_PER_STEP]
    keys_ref, keysnew_ref, thr_ref, scr = rest[PAGES_PER_STEP:]
    s = pl.program_id(1)
    n_steps = pl.num_programs(1)
    qit = qit_ref[...]
    w = w_ref[...] * IDX_SCALE
    rows_per_step = PAGES_PER_STEP * PAGE_SIZE
    n_past = scr.shape[0] - PAGE_SIZE

    def score(kib):
        d = jnp.dot(kib.astype(BF16), qit, preferred_element_type=F32)
        tot = _lane_group_allsum(jnp.maximum(d, 0.0) * w) + 0.0
        return _sortable(tot)

    for i, page in enumerate(pages):
        key = score(page[...])
        keys_ref[i * PAGE_SIZE:(i + 1) * PAGE_SIZE, :] = key
        scr[pl.ds(pl.multiple_of(s * rows_per_step, rows_per_step) + i * PAGE_SIZE, PAGE_SIZE), :] = key

    @pl.when(s == n_steps - 1)
    def _finish():
        key = score(kinew_ref[...])
        row = lax.broadcasted_iota(I32, key.shape, 0)
        qidx = lax.broadcasted_iota(I32, key.shape, 1) % n_new
        key = jnp.where((row < n_new) & (row <= qidx), key, INT_MIN)
        keysnew_ref[...] = key
        scr[n_past:n_past + PAGE_SIZE, :] = key
        blk = 1024

        def count_ge(cand):
            def body(kb, acc):
                c = scr[pl.ds(pl.multiple_of(kb * blk, blk), blk), :]
                return acc + jnp.sum(jnp.where(c >= cand, 1.0, 0.0), axis=0, keepdims=True)
            acc = lax.fori_loop(0, n_past // blk, body, jnp.zeros((1, LANES), F32))
            return acc + jnp.sum(jnp.where(scr[n_past:n_past + PAGE_SIZE, :] >= cand, 1.0, 0.0),
                                 axis=0, keepdims=True)

        thr = _kth_largest_key(count_ge, n_sel, (1, LANES))
        thr = jnp.maximum(thr, INT_MIN + 1)
        thr_ref[...] = jnp.broadcast_to(thr, thr_ref.shape)


def _page_spec(width, k):
    return pl.BlockSpec((None, PAGE_SIZE, width), lambda b, s, pt: (pt[b, s * PAGES_PER_STEP + k], 0, 0))


def _sample_score(page_table, qi_t, w_t, ki_new, cache_ki, n_sel, n_new):
    nb, n_pages = page_table.shape
    n_steps = n_pages // PAGES_PER_STEP
    per_b = lambda b, s, pt: (b, 0, 0)
    grid_spec = pltpu.PrefetchScalarGridSpec(
        num_scalar_prefetch=1,
        grid=(nb, n_steps),
        in_specs=[
            pl.BlockSpec((None, IDX_DIM, LANES), per_b),
            pl.BlockSpec((None, 1, LANES), per_b),
            pl.BlockSpec((None, PAGE_SIZE, IDX_DIM), per_b),
        ] + [_page_spec(IDX_DIM, k) for k in range(PAGES_PER_STEP)],
        out_specs=[
            pl.BlockSpec((None, PAGES_PER_STEP * PAGE_SIZE, LANES), lambda b, s, pt: (b, s, 0)),
            pl.BlockSpec((None, PAGE_SIZE, LANES), per_b),
            pl.BlockSpec((None, 8, LANES), per_b),
        ],
        scratch_shapes=[pltpu.VMEM((n_pages * PAGE_SIZE + PAGE_SIZE, LANES), I32)],
    )
    return pl.pallas_call(
        functools.partial(_sample_score_kernel, n_sel=n_sel, n_new=n_new),
        grid_spec=grid_spec,
        out_shape=[
            jax.ShapeDtypeStruct((nb, n_pages * PAGE_SIZE, LANES), I32),
            jax.ShapeDtypeStruct((nb, PAGE_SIZE, LANES), I32),
            jax.ShapeDtypeStruct((nb, 8, LANES), I32),
        ],
        compiler_params=_params("parallel", "arbitrary"),
        name="sample_score",
    )(page_table, qi_t, w_t, ki_new, *([cache_ki] * PAGES_PER_STEP))


def _sample_attend_kernel(pt_ref, qbd_ref, keys_ref, keysnew_ref, thr_ref, knew_ref, vnew_ref,
                          diag_ref, *rest):
    kpages = rest[:PAGES_PER_STEP]
    vpages = rest[PAGES_PER_STEP:2 * PAGES_PER_STEP]
    out_ref, m_ref, l_ref, acc_ref = rest[2 * PAGES_PER_STEP:]
    s = pl.program_id(1)
    n_steps = pl.num_programs(1)

    @pl.when(s == 0)
    def _init():
        m_ref[...] = jnp.full(m_ref.shape, NEG_BIG, F32)
        l_ref[...] = jnp.zeros(l_ref.shape, F32)
        acc_ref[...] = jnp.zeros(acc_ref.shape, F32)

    qbd = qbd_ref[...]
    thr = thr_ref[0:1, :]

    def process(kp, vp, keyblk):
        lg = jnp.dot(kp.astype(BF16), qbd, preferred_element_type=F32) * ATTN_SCALE
        lm = jnp.where(keyblk >= thr, lg, NEG_BIG)
        m = m_ref[...]
        m_new = jnp.maximum(m, jnp.max(lm, axis=0, keepdims=True))
        pe = jnp.exp(lm - m_new)
        alpha = jnp.exp(m - m_new)
        l_ref[...] = alpha * l_ref[...] + jnp.sum(pe, axis=0, keepdims=True)
        pv = jnp.dot(vp.T.astype(BF16), pe.astype(BF16), preferred_element_type=F32)
        acc_ref[...] = alpha * acc_ref[...] + pv
        m_ref[...] = m_new

    for i in range(PAGES_PER_STEP):
        process(kpages[i][...], vpages[i][...], keys_ref[i * PAGE_SIZE:(i + 1) * PAGE_SIZE, :])

    @pl.when(s == n_steps - 1)
    def _finish():
        process(knew_ref[...], vnew_ref[...], keysnew_ref[...])
        out_ref[...] = _lane_group_allsum(acc_ref[...] / l_ref[...] * diag_ref[...])


def _sample_attend(page_table, qbd, keys, keys_new, thr, k_new, v_new, diag, cache_k, cache_v):
    nb, n_pages = page_table.shape
    n_steps = n_pages // PAGES_PER_STEP
    per_b = lambda b, s, pt: (b, 0, 0)
    grid_spec = pltpu.PrefetchScalarGridSpec(
        num_scalar_prefetch=1,
        grid=(nb, n_steps),
        in_specs=[
            pl.BlockSpec((None, ATTN_WIDTH, LANES), per_b),
            pl.BlockSpec((None, PAGES_PER_STEP * PAGE_SIZE, LANES), lambda b, s, pt: (b, s, 0)),
            pl.BlockSpec((None, PAGE_SIZE, LANES), per_b),
            pl.BlockSpec((None, 8, LANES), per_b),
            pl.BlockSpec((None, PAGE_SIZE, ATTN_WIDTH), per_b),
            pl.BlockSpec((None, PAGE_SIZE, ATTN_WIDTH), per_b),
            pl.BlockSpec((ATTN_WIDTH, LANES), lambda b, s, pt: (0, 0)),
        ] + [_page_spec(ATTN_WIDTH, k) for k in range(PAGES_PER_STEP)] * 2,
        out_specs=pl.BlockSpec((None, ATTN_WIDTH, LANES), per_b),
        scratch_shapes=[pltpu.VMEM((1, LANES), F32), pltpu.VMEM((1, LANES), F32),
                        pltpu.VMEM((ATTN_WIDTH, LANES), F32)],
    )
    return pl.pallas_call(
        _sample_attend_kernel,
        grid_spec=grid_spec,
        out_shape=jax.ShapeDtypeStruct((nb, ATTN_WIDTH, LANES), F32),
        compiler_params=_params("parallel", "arbitrary"),
        name="sample_attend",
    )(page_table, qbd, keys, keys_new, thr, k_new, v_new, diag,
      *([cache_k] * PAGES_PER_STEP), *([cache_v] * PAGES_PER_STEP))


def _mix_kernel(a_ref, u_ref, gv_ref, x_ref, wsp_ref, bias_ref, wo_ref, g_ref, x1_ref, hp_ref):
    gvb = gv_ref[...].astype(BF16)
    group = lax.broadcasted_iota(I32, gvb.shape, 1) // HEAD_DIM
    mixed = bias_ref[...]
    for g in range(N_GROUPS):
        r = jnp.dot(wsp_ref[g], gvb, preferred_element_type=F32)
        mixed = mixed + jnp.where(group == g, r, 0.0)
    gm = (u_ref[...] * mixed).astype(BF16)
    mix = (jnp.dot(a_ref[...], wo_ref[:ATTN_WIDTH, :], preferred_element_type=F32)
           + jnp.dot(gm, wo_ref[ATTN_WIDTH:, :], preferred_element_type=F32))
    x1 = x_ref[...] + mix
    x1_ref[...] = x1
    hp_ref[...] = _rms(x1, g_ref[...]).astype(BF16)


def _mix(a, u, gv, x, wsp, bias, wo, g):
    n, d = x.shape
    row = lambda i: (i, 0)
    fixed = lambda i: (0, 0)
    return pl.pallas_call(
        _mix_kernel,
        grid=(n // CHUNK,),
        in_specs=[
            pl.BlockSpec((CHUNK, ATTN_WIDTH), row),
            pl.BlockSpec((CHUNK, GMLP_WIDTH), row),
            pl.BlockSpec((CHUNK, GMLP_WIDTH), row),
            pl.BlockSpec((CHUNK, d), row),
            pl.BlockSpec(wsp.shape, lambda i: (0, 0, 0)),
            pl.BlockSpec(bias.shape, fixed),
            pl.BlockSpec(wo.shape, fixed),
            pl.BlockSpec((1, d), fixed),
        ],
        out_specs=[pl.BlockSpec((CHUNK, d), row), pl.BlockSpec((CHUNK, d), row)],
        out_shape=[jax.ShapeDtypeStruct((n, d), F32), jax.ShapeDtypeStruct((n, d), BF16)],
        compiler_params=_params("parallel"),
        name="mix",
    )(a, u, gv, x, wsp, bias, wo, g)


def _top16(s):
    rows, n = s.shape
    ridx = lax.broadcasted_iota(I32, s.shape, 0).astype(F32)
    r16 = lax.broadcasted_iota(I32, (PEER_TOPK, n), 0)

    def body(r, carry):
        cur, vals = carry
        mx = jnp.max(cur, axis=0, keepdims=True)
        first = jnp.min(jnp.where(cur == mx, ridx, float(rows)), axis=0, keepdims=True)
        cur = jnp.where(ridx == first, -jnp.inf, cur)
        vals = jnp.where(r16 == r, mx, vals)
        return cur, vals

    cur, vals = lax.fori_loop(0, PEER_TOPK, body, (s, jnp.zeros((PEER_TOPK, n), F32)))
    return vals, cur != s


def _peer_kernel(hp_ref, x1_ref, wpqt_ref, sk_ref, u_ref, vt_ref, gf_ref, y_ref,
                 s1_ref, s2_ref, e1_ref, e2_ref, tau_ref, out_ref, *, tm, eb):
    e = pl.program_id(1)
    n_e = pl.num_programs(1)
    hp = hp_ref[...]
    nt = (((1,), (1,)), ((), ()))

    @pl.when(e == 0)
    def _select():
        qt = lax.dot_general(wpqt_ref[...], hp, nt, preferred_element_type=F32).astype(BF16)
        for hh in range(N_PEER_HEADS):
            for c, ref in enumerate((s1_ref, s2_ref)):
                idx = hh * 2 + c
                ref[hh] = jnp.dot(sk_ref[idx], qt[idx * PEER_HALF:(idx + 1) * PEER_HALF, :],
                                  preferred_element_type=F32)

        def head_body(hh, carry):
            for ch in range(tm // LANES):
                sl = slice(ch * LANES, (ch + 1) * LANES)
                s1 = s1_ref[hh, :, sl]
                s2 = s2_ref[hh, :, sl]
                a, m1 = _top16(s1)
                b, m2 = _top16(s2)
                cand = jnp.concatenate([a[r:r + 1, :] + b for r in range(PEER_TOPK)], axis=0)
                best, _ = _top16(cand)
                z = jnp.sum(jnp.exp(best - best[0:1, :]), axis=0, keepdims=True)
                e1_ref[hh, :, sl] = jnp.where(m1, jnp.exp(s1 - a[0:1, :]), 0.0)
                e2_ref[hh, :, sl] = jnp.where(m2, jnp.exp(s2 - b[0:1, :]), 0.0) / z
                tau_ref[hh, :, sl] = jnp.broadcast_to(best[PEER_TOPK - 1:PEER_TOPK, :], (8, LANES))
            return carry

        lax.fori_loop(0, N_PEER_HEADS, head_body, 0)
        out_ref[...] = jnp.zeros(out_ref.shape, F32)

    act = _gelu(lax.dot_general(u_ref[...], hp, nt, preferred_element_type=F32))
    parts = []
    for ii in range(eb // N_KEYS):
        i = e * (eb // N_KEYS) + ii
        wt = jnp.zeros((N_KEYS, tm), F32)
        for hh in range(N_PEER_HEADS):
            cs = s1_ref[hh, pl.ds(i, 1), :] + s2_ref[hh]
            wt = wt + jnp.where(cs >= tau_ref[hh, 0:1, :],
                                e1_ref[hh, pl.ds(i, 1), :] * e2_ref[hh], 0.0)
        parts.append((wt * act[ii * N_KEYS:(ii + 1) * N_KEYS, :]).astype(BF16))
    gt = jnp.concatenate(parts, axis=0)
    out_ref[...] += jnp.dot(vt_ref[...], gt, preferred_element_type=F32)

    @pl.when(e == n_e - 1)
    def _finish():
        y_ref[...] = _rms(x1_ref[...] + out_ref[...].T, gf_ref[...])


def _peer(hp, x1, wpqt, sk, eu, evt, gf, tm, eb):
    n, d = x1.shape
    n_exp = eu.shape[0]
    row = lambda t, e: (t, 0)
    fixed = lambda t, e: (0, 0)
    head_scr = pltpu.VMEM((N_PEER_HEADS, N_KEYS, tm), F32)
    return pl.pallas_call(
        functools.partial(_peer_kernel, tm=tm, eb=eb),
        grid=(n // tm, n_exp // eb),
        in_specs=[
            pl.BlockSpec((tm, d), row),
            pl.BlockSpec((tm, d), row),
            pl.BlockSpec(wpqt.shape, fixed),
            pl.BlockSpec(sk.shape, lambda t, e: (0, 0, 0)),
            pl.BlockSpec((eb, d), lambda t, e: (e, 0)),
            pl.BlockSpec((d, eb), lambda t, e: (0, e)),
            pl.BlockSpec((1, d), fixed),
        ],
        out_specs=pl.BlockSpec((tm, d), row),
        out_shape=jax.ShapeDtypeStruct((n, d), F32),
        scratch_shapes=[head_scr, head_scr, head_scr, head_scr,
                        pltpu.VMEM((N_PEER_HEADS, 8, tm), F32), pltpu.VMEM((d, tm), F32)],
        compiler_params=_params("parallel", "arbitrary"),
        name="peer",
    )(hp, x1, wpqt, sk, eu, evt, gf)


def _split_w_in(w_in_l):
    a = 3 * ATTN_WIDTH + N_IDX_HEADS * IDX_DIM
    b = a + IDX_DIM + N_IDX_HEADS
    wm = jnp.concatenate([w_in_l[:, :a], w_in_l[:, b:]], axis=1).astype(BF16)
    ws = jnp.pad(w_in_l[:, a:b], ((0, 0), (0, LANES - (b - a)))).astype(BF16)
    return wm, ws


def _prompt_dsa_operands(q, k, v, qi, small, b, t, tq, tk):
    nq = t // tq
    qi_t = qi.reshape(b, nq, tq, N_IDX_HEADS, IDX_DIM).transpose(0, 1, 4, 3, 2)
    qi_t = qi_t.reshape(b, nq, IDX_DIM, N_IDX_HEADS * tq).astype(BF16)
    wi = small[:, IDX_DIM:IDX_DIM + N_IDX_HEADS]
    w_t = wi.reshape(b, nq, tq, N_IDX_HEADS).transpose(0, 1, 3, 2).reshape(b, nq, 1, N_IDX_HEADS * tq)
    ki = small[:, :IDX_DIM].reshape(b, t, IDX_DIM).astype(BF16)
    qt = q.astype(BF16).reshape(b, nq, tq, N_HEADS // 2, 2, HEAD_DIM).transpose(0, 1, 3, 4, 5, 2)
    eye = jnp.eye(2, dtype=BF16)
    qbd = qt[:, :, :, :, :, None, :] * eye[None, None, None, :, None, :, None]
    qbd = qbd.reshape(b, nq, N_HEADS // 2, 2 * HEAD_DIM, 2 * tq)
    kb = k.astype(BF16).reshape(b, t, ATTN_WIDTH)
    vt = v.astype(BF16).reshape(b, t // tk, tk, ATTN_WIDTH).transpose(0, 1, 3, 2)
    return qi_t, w_t, ki, qbd, kb, vt


def _sample_group_attention(qs, ks, vs, qis, smalls, page_table, cache_k_l, cache_v_l, cache_ki_l, bd, tn):
    past = page_table.shape[1] * PAGE_SIZE
    lane_pad = LANES - N_HEADS * tn
    qi_st = qis.reshape(bd, tn, N_IDX_HEADS, IDX_DIM).transpose(0, 3, 2, 1).reshape(bd, IDX_DIM, N_IDX_HEADS * tn)
    qi_st = jnp.pad(qi_st, ((0, 0), (0, 0), (0, lane_pad))).astype(BF16)
    w_st = smalls[:, IDX_DIM:IDX_DIM + N_IDX_HEADS].reshape(bd, tn, N_IDX_HEADS).transpose(0, 2, 1)
    w_st = jnp.pad(w_st.reshape(bd, 1, N_IDX_HEADS * tn), ((0, 0), (0, 0), (0, lane_pad)))
    row_pad = ((0, 0), (0, PAGE_SIZE - tn), (0, 0))
    ki_new = jnp.pad(smalls[:, :IDX_DIM].reshape(bd, tn, IDX_DIM), row_pad)
    n_sel = min(MAX_TOPK, (past + tn) // 4)
    keys, keys_new, thr = _sample_score(page_table, qi_st, w_st, ki_new, cache_ki_l, n_sel, tn)
    qbd_s = jnp.einsum('bthd,hg->bhdgt', qs.reshape(bd, tn, N_HEADS, HEAD_DIM), jnp.eye(N_HEADS, dtype=F32))
    qbd_s = jnp.pad(qbd_s.reshape(bd, ATTN_WIDTH, N_HEADS * tn), ((0, 0), (0, 0), (0, lane_pad))).astype(BF16)
    k_new = jnp.pad(ks.reshape(bd, tn, ATTN_WIDTH), row_pad)
    v_new = jnp.pad(vs.reshape(bd, tn, ATTN_WIDTH), row_pad)
    diag = ((jnp.arange(ATTN_WIDTH)[:, None] // HEAD_DIM) == (jnp.arange(LANES)[None, :] // tn)).astype(F32)
    n_pool = cache_k_l.shape[0]
    a_st = _sample_attend(page_table, qbd_s, keys, keys_new, thr, k_new, v_new, diag,
                          cache_k_l.reshape(n_pool, PAGE_SIZE, ATTN_WIDTH),
                          cache_v_l.reshape(n_pool, PAGE_SIZE, ATTN_WIDTH))
    return a_st[:, :, :tn].transpose(0, 2, 1).reshape(bd * tn, ATTN_WIDTH)


def _sample_gmlp_weights(w_sp_l, bias_p, tn):
    per_chunk = CHUNK // tn
    tril = jnp.tril(jnp.ones((tn, tn), dtype=bool))
    wsp_s = jnp.einsum('ij,gts->gitjs', jnp.eye(per_chunk, dtype=F32),
                       jnp.where(tril[None], w_sp_l[:, :tn, :tn], 0))
    wsp_s = wsp_s.reshape(N_GROUPS, CHUNK, CHUNK).astype(BF16)
    return wsp_s, jnp.tile(bias_p[:tn], (per_chunk, 1))


def kernel(x_prompt, x_sample, cache_k, cache_v, cache_kidx, page_table, norm_mix_g, w_in,
           gv_norm_g, w_sp, b_sp, w_out, norm_ffn_g, w_pq, peer_sub_keys, expert_u, expert_v,
           norm_final_g):
    b, t, d = x_prompt.shape
    bd, tn, _ = x_sample.shape
    depth = w_in.shape[0]
    assert depth == 1 and tn == 8 and t % 256 == 0 and (bd * tn) % CHUNK == 0
    n_pages = page_table.shape[1]
    past = n_pages * PAGE_SIZE
    l = 0

    wm, ws = _split_w_in(w_in[l])
    g_mix = norm_mix_g[l].reshape(1, d)
    gvg = gv_norm_g[l].reshape(1, GMLP_WIDTH)
    lane_group = jnp.arange(GMLP_WIDTH) // HEAD_DIM
    gmean = (lane_group[:, None] == lane_group[None, :]).astype(BF16) * (1.0 / HEAD_DIM)
    invf = ROPE_THETA ** (-jnp.arange(ROPE_HALF, dtype=F32) / ROPE_HALF)
    invf = jnp.tile(invf, LANES // ROPE_HALF).reshape(1, LANES)
    tril = jnp.tril(jnp.ones((CHUNK, CHUNK), dtype=bool))
    wsp_p = jnp.where(tril[None], w_sp[l], 0).astype(BF16)
    bias_p = jnp.repeat(b_sp[l].T, HEAD_DIM, axis=1)
    wsp_s, bias_s = _sample_gmlp_weights(w_sp[l], bias_p, tn)
    wo =w_out[l].astype(BF16)
    g_ffn = norm_ffn_g[l].reshape(1, d)
    wpqt = w_pq[l].T.astype(BF16)
    sk = peer_sub_keys[l].reshape(2 * N_PEER_HEADS, N_KEYS, PEER_HALF).astype(BF16)
    eu = expert_u[l].astype(BF16)
    evt = expert_v[l].T.astype(BF16)
    gf = norm_final_g.reshape(1, d)

    n_p = b * t
    pos_p = jnp.tile(jnp.arange(t, dtype=F32), b).reshape(n_p, 1)
    q, k, v, qi, u, gv, small = _project(x_prompt.reshape(n_p, d), pos_p, invf, g_mix, wm, ws,
                                         gvg, gmean, 256)
    tq, tk = 256, 128
    ops = _prompt_dsa_operands(q, k, v, qi, small, b, t, tq, tk)
    a_t = _dsa_prompt(*ops, tq, tk, min(MAX_TOPK, t // 4))
    a_p = a_t.transpose(0, 2, 1).reshape(n_p, ATTN_WIDTH).astype(BF16)
    x1_p, hp_p = _mix(a_p, u, gv, x_prompt.reshape(n_p, d), wsp_p, bias_p, wo, g_ffn)

    n_s = bd * tn
    pos_s = jnp.tile(past + jnp.arange(tn, dtype=F32), bd).reshape(n_s, 1)
    qs, ks, vs, qis, us, gvs, smalls = _project(x_sample.reshape(n_s, d), pos_s, invf, g_mix, wm,
                                                ws, gvg, gmean, n_s)
    a_s = _sample_group_attention(qs, ks, vs, qis, smalls, page_table, cache_k[l], cache_v[l],
                                  cache_kidx[l], bd, tn).astype(BF16)
    x1_s, hp_s = _mix(a_s, us, gvs, x_sample.reshape(n_s, d), wsp_s, bias_s, wo, g_ffn)

    x1 = jnp.concatenate([x1_p, x1_s], axis=0)
    hp = jnp.concatenate([hp_p, hp_s], axis=0)
    n_all = n_p + n_s
    tm = 640 if n_all % 640 == 0 else 128
    y = _peer(hp, x1, wpqt, sk, eu, evt, gf, tm, 1024)

    heads = (N_HEADS, HEAD_DIM)
    return (y[:n_p].reshape(b, t, d),
            y[n_p:].reshape(bd, tn, d),
            k.reshape(1, b, t, *heads),
            v.reshape(1, b, t, *heads),
            small[:, :IDX_DIM].reshape(1, b, t, IDX_DIM),
            ks.reshape(1, bd, tn, *heads),
            vs.reshape(1, bd, tn, *heads),
            smalls[:, :IDX_DIM].reshape(1, bd, tn, IDX_DIM),
            gvs.reshape(1, bd, tn, N_GROUPS, HEAD_DIM))
```

```python
import functools

import jax
import jax.numpy as jnp
from jax import lax
from jax.experimental import pallas as pl
from jax.experimental.pallas import tpu as pltpu

F32 = jnp.float32
BF16 = jnp.bfloat16
I32 = jnp.int32

EPS = 1e-6
ROPE_THETA = 10000.0
HEAD_DIM = 64
ROPE_HALF = HEAD_DIM // 2
N_HEADS = 8
ATTN_WIDTH = N_HEADS * HEAD_DIM
N_IDX_HEADS = 8
IDX_DIM = 64
MAX_TOPK = 256
N_GROUPS = 8
GMLP_WIDTH = N_GROUPS * HEAD_DIM
CHUNK = 128
PAGE_SIZE = 128
N_PEER_HEADS = 8
N_KEYS = 128
PEER_HALF = 64
PEER_TOPK = 16
LANES = 128

INT_MIN = -(2 ** 31)
NEG_BIG = -1e30
IDX_SCALE = (IDX_DIM ** -0.5) * (N_IDX_HEADS ** -0.5)
ATTN_SCALE = HEAD_DIM ** -0.5
LOG2E = 1.4426950408889634

VMEM_LIMIT_BYTES = 56 * 1024 * 1024


def _params(*sem):
    return pltpu.CompilerParams(dimension_semantics=sem, vmem_limit_bytes=VMEM_LIMIT_BYTES)


def _gelu(x):
    cdf = 0.5 * (1.0 + jnp.tanh(0.7978845608028654 * (x + 0.044715 * (x * x * x))))
    return x * cdf


def _rms(x, g):
    return x * lax.rsqrt(jnp.mean(x * x, axis=-1, keepdims=True) + EPS) * g


def _split3(x):
    a = x.astype(BF16)
    r = x - a.astype(F32)
    b = r.astype(BF16)
    c = (r - b.astype(F32)).astype(BF16)
    return a, b, c


def _dot_f32_by_bf16(x, m):
    a, b, c = _split3(x)
    d = lambda p: jnp.dot(p, m, preferred_element_type=F32)
    return d(a) + d(b) + d(c)


def _sortable(x):
    bits = lax.bitcast_convert_type(x, I32)
    return jnp.where(bits < 0, bits ^ 0x7FFFFFFF, bits)


COUNT_ROWS = 64


def _count_partial(mask):
    rows, n = mask.shape
    ones = jnp.where(mask, 1.0, 0.0)
    return jnp.sum(ones.reshape(rows // COUNT_ROWS, COUNT_ROWS, n), axis=0)


def _kth_largest_key(count_ge, k, shape):
    c0 = count_ge(jnp.zeros(shape, I32))
    t0 = jnp.where(c0 >= k, 0, INT_MIN).astype(I32)

    def body(i, t):
        cand = t + lax.shift_left(jnp.int32(1), 30 - i)
        return jnp.where(count_ge(cand) >= k, cand, t)

    return lax.fori_loop(0, 31, body, t0)


def _project_kernel(x_ref, pos_ref, invf_ref, g_ref, wm_ref, ws_ref, gvg_ref, gmean_ref,
                    q_ref, k_ref, v_ref, qi_ref, u_ref, gv_ref, small_ref):
    h = _rms(x_ref[...], g_ref[...]).astype(BF16)
    ang = pos_ref[...] * invf_ref[...]
    cos = jnp.cos(ang)
    sin = jnp.sin(ang)
    lane = lax.broadcasted_iota(I32, ang.shape, 1)
    first = (lane % HEAD_DIM) < ROPE_HALF
    sin_signed = jnp.where(first, -sin, sin)

    def rope(xc):
        partner = jnp.where(first, pltpu.roll(xc, LANES - ROPE_HALF, 1), pltpu.roll(xc, ROPE_HALF, 1))
        return xc * cos + partner * sin_signed

    def proj(c0):
        return jnp.dot(h, wm_ref[:, c0:c0 + ATTN_WIDTH], preferred_element_type=F32)

    for ref, c0 in ((q_ref, 0), (k_ref, ATTN_WIDTH), (qi_ref, 3 * ATTN_WIDTH)):
        p = proj(c0)
        for c in range(ATTN_WIDTH // LANES):
            ref[:, c * LANES:(c + 1) * LANES] = rope(p[:, c * LANES:(c + 1) * LANES])
    v_ref[...] = proj(2 * ATTN_WIDTH)
    u_ref[...] = _gelu(proj(4 * ATTN_WIDTH))
    gl = _gelu(proj(5 * ATTN_WIDTH))
    ms = _dot_f32_by_bf16(gl * gl, gmean_ref[...])
    gv_ref[...] = gl * lax.rsqrt(ms + EPS) * gvg_ref[...]
    sm = jnp.dot(h, ws_ref[...], preferred_element_type=F32)
    small_ref[...] = jnp.where(lane < IDX_DIM, rope(sm), sm)


def _project(x2d, pos, invf, g, wm, ws, gvg, gmean, tm):
    n, d = x2d.shape
    row = lambda i: (i, 0)
    fixed = lambda i: (0, 0)
    wide = jax.ShapeDtypeStruct((n, ATTN_WIDTH), F32)
    return pl.pallas_call(
        _project_kernel,
        grid=(n // tm,),
        in_specs=[
            pl.BlockSpec((tm, d), row),
            pl.BlockSpec((tm, 1), row),
            pl.BlockSpec((1, LANES), fixed),
            pl.BlockSpec((1, d), fixed),
            pl.BlockSpec(wm.shape, fixed),
            pl.BlockSpec(ws.shape, fixed),
            pl.BlockSpec((1, GMLP_WIDTH), fixed),
            pl.BlockSpec(gmean.shape, fixed),
        ],
        out_specs=[pl.BlockSpec((tm, ATTN_WIDTH), row)] * 6 + [pl.BlockSpec((tm, LANES), row)],
        out_shape=[wide] * 6 + [jax.ShapeDtypeStruct((n, LANES), F32)],
        compiler_params=_params("parallel"),
        name="project",
    )(x2d, pos, invf, g, wm, ws, gvg, gmean)


def _dsa_prompt_kernel(qi_ref, w_ref, ki_ref, qbd_ref, k_ref, vt_ref, out_ref, keys_ref, acc_ref, lg_ref,
                       *, tq, tk, n_sel):
    j = pl.program_id(1)
    n_tiles = pl.num_programs(1)
    sub = tq // tk
    nkb = (j + 1) * sub
    q0 = j * tq
    qi = qi_ref[0, 0]
    w = w_ref[0, 0] * IDX_SCALE
    qpos = q0 + lax.broadcasted_iota(I32, (tk, tq), 1)
    krow = lax.broadcasted_iota(I32, (tk, tq), 0)

    def score_body(kq, carry):
        for u in range(sub):
            r0 = pl.multiple_of(kq * tq, tq) + u * tk
            d = jnp.dot(ki_ref[0, pl.ds(r0, tk), :], qi, preferred_element_type=F32)
            s = jnp.zeros((tk, tq), F32)
            for h in range(N_IDX_HEADS):
                s = s + jnp.maximum(d[:, h * tq:(h + 1) * tq], 0.0) * w[:, h * tq:(h + 1) * tq]
            keys_ref[pl.ds(r0, tk), :] = jnp.where(krow + r0 <= qpos, _sortable(s), INT_MIN)
        return carry

    lax.fori_loop(0, j + 1, score_body, 0)

    @pl.when(jnp.logical_and((j + 1) % 2 == 1, j + 1 < n_tiles))
    def _blank():
        keys_ref[pl.ds(pl.multiple_of((j + 1) * tq, tq), tq), :] = jnp.full((tq, tq), INT_MIN, I32)

    def count_ge(cand):
        def body(kb, acc):
            blk = keys_ref[pl.ds(pl.multiple_of(kb * 2 * tq, 2 * tq), 2 * tq), :]
            return acc + _count_partial(blk >= cand)
        acc = lax.fori_loop(0, (j + 2) // 2, body, jnp.zeros((COUNT_ROWS, tq), F32))
        return jnp.sum(acc, axis=0, keepdims=True)

    thr = _kth_largest_key(count_ge, n_sel, (1, tq))
    thr = jnp.maximum(thr, INT_MIN + 1)

    n_pairs = N_HEADS // 2
    acc_ref[...] = jnp.zeros(acc_ref.shape, F32)

    def logits(kb):
        r0 = pl.multiple_of(kb * tk, tk)
        return [jnp.dot(k_ref[0, pl.ds(r0, tk), p * LANES:(p + 1) * LANES], qbd_ref[0, 0, p],
                        preferred_element_type=F32) for p in range(n_pairs)]

    def stash(slot, lgs):
        for p in range(n_pairs):
            lg_ref[slot, :, p * 2 * tq:(p + 1) * 2 * tq] = lgs[p]

    stash(0, logits(0))

    def att_body(kb, carry):
        ms, ls = carry
        r0 = pl.multiple_of(kb * tk, tk)
        slot = kb % 2
        nxt = logits(jnp.minimum(kb + 1, nkb - 1))
        sel = keys_ref[pl.ds(r0, tk), :] >= thr
        new_ms, new_ls = [], []
        for p in range(n_pairs):
            rows = slice(p * LANES, (p + 1) * LANES)
            lg = lg_ref[slot, :, p * 2 * tq:(p + 1) * 2 * tq] * (ATTN_SCALE * LOG2E)
            lm = jnp.concatenate([jnp.where(sel, lg[:, :tq], NEG_BIG),
                                  jnp.where(sel, lg[:, tq:], NEG_BIG)], axis=1)
            m_new = jnp.maximum(ms[p], jnp.max(lm, axis=0, keepdims=True))
            pe = jnp.exp2(lm - m_new)
            alpha = jnp.exp2(ms[p] - m_new)
            new_ls.append(alpha * ls[p] + jnp.sum(pe, axis=0, keepdims=True))
            new_ms.append(m_new)
            pb = pe.astype(BF16)
            vtb = vt_ref[0, kb, rows, :]
            pv0 = jnp.dot(vtb[:HEAD_DIM], pb[:, :tq], preferred_element_type=F32)
            pv1 = jnp.dot(vtb[HEAD_DIM:], pb[:, tq:], preferred_element_type=F32)
            acc = acc_ref[rows, :]
            acc_ref[rows, :] = jnp.concatenate([alpha[:, :tq] * acc[:HEAD_DIM] + pv0,
                                                alpha[:, tq:] * acc[HEAD_DIM:] + pv1], axis=0)
        stash(1 - slot, nxt)
        return tuple(new_ms), tuple(new_ls)

    init = (tuple(jnp.full((1, 2 * tq), NEG_BIG, F32) for _ in range(n_pairs)),
            tuple(jnp.zeros((1, 2 * tq), F32) for _ in range(n_pairs)))
    _, ls = lax.fori_loop(0, nkb, att_body, init)
    for p in range(n_pairs):
        rows = slice(p * LANES, (p + 1) * LANES)
        acc = acc_ref[rows, :]
        out_ref[0, rows, :] = jnp.concatenate(
            [acc[:HEAD_DIM] / ls[p][:, :tq], acc[HEAD_DIM:] / ls[p][:, tq:]], axis=0)


def _dsa_prompt(qi_t, w_t, ki, qbd, k, vt, tq, tk, n_sel):
    b, nq = qi_t.shape[:2]
    t = k.shape[1]
    tile = lambda bb, j: (bb, j, 0, 0)
    whole3 = lambda bb, j: (bb, 0, 0)
    return pl.pallas_call(
        functools.partial(_dsa_prompt_kernel, tq=tq, tk=tk, n_sel=n_sel),
        grid=(b, nq),
        in_specs=[
            pl.BlockSpec((1, 1, IDX_DIM, N_IDX_HEADS * tq), tile),
            pl.BlockSpec((1, 1, 1, N_IDX_HEADS * tq), tile),
            pl.BlockSpec((1, t, IDX_DIM), whole3, pipeline_mode=pl.Buffered(1)),
            pl.BlockSpec((1, 1, N_HEADS // 2, LANES, 2 * tq), lambda bb, j: (bb, j, 0, 0, 0)),
            pl.BlockSpec((1, t, ATTN_WIDTH), whole3, pipeline_mode=pl.Buffered(1)),
            pl.BlockSpec((1, t // tk, ATTN_WIDTH, tk), lambda bb, j: (bb, 0, 0, 0),
                         pipeline_mode=pl.Buffered(1)),
        ],
        out_specs=pl.BlockSpec((1, ATTN_WIDTH, tq), lambda bb, j: (bb, 0, j)),
        out_shape=jax.ShapeDtypeStruct((b, ATTN_WIDTH, t), F32),
        scratch_shapes=[pltpu.VMEM((t, tq), I32), pltpu.VMEM((ATTN_WIDTH, tq), F32),
                        pltpu.VMEM((2, tk, N_HEADS * tq), F32)],
        compiler_params=_params("parallel", "parallel"),
        name="dsa_prompt",
    )(qi_t, w_t, ki, qbd, k, vt)


PAGES_PER_STEP = 8


def _lane_group_allsum(x):
    for sh in (64, 32, 16, 8):
        x = x + pltpu.roll(x, sh, 1)
    return x


def _sample_score_kernel(pt_ref, qit_ref, w_ref, kinew_ref, *rest, n_sel, n_new):
    pages = rest[:PAGES_PER_STEP]
    keys_ref, keysnew_ref, thr_ref, scr = rest[PAGES_PER_STEP:]
    s = pl.program_id(1)
    n_steps = pl.num_programs(1)
    qit = qit_ref[...]
    w = w_ref[...] * IDX_SCALE
    rows_per_step = PAGES_PER_STEP * PAGE_SIZE
    n_past = scr.shape[0] - PAGE_SIZE

    def score(kib):
        d = jnp.dot(kib.astype(BF16), qit, preferred_element_type=F32)
        tot = _lane_group_allsum(jnp.maximum(d, 0.0) * w) + 0.0
        return _sortable(tot)

    for i, page in enumerate(pages):
        key = score(page[...])
        keys_ref[i * PAGE_SIZE:(i + 1) * PAGE_SIZE, :] = key
        scr[pl.ds(pl.multiple_of(s * rows_per_step, rows_per_step) + i * PAGE_SIZE, PAGE_SIZE), :] = key

    @pl.when(s == n_steps - 1)
    def _finish():
        key = score(kinew_ref[...])
        row = lax.broadcasted_iota(I32, key.shape, 0)
        qidx = lax.broadcasted_iota(I32, key.shape, 1) % n_new
        key = jnp.where((row < n_new) & (row <= qidx), key, INT_MIN)
        keysnew_ref[...] = key
        scr[n_past:n_past + PAGE_SIZE, :] = key
        blk = 1024

        def count_ge(cand):
            def body(kb, acc):
                c = scr[pl.ds(pl.multiple_of(kb * blk, blk), blk), :]
                return acc + _count_partial(c >= cand)
            acc = lax.fori_loop(0, n_past // blk, body, jnp.zeros((COUNT_ROWS, LANES), F32))
            acc = acc + _count_partial(scr[n_past:n_past + PAGE_SIZE, :] >= cand)
            return jnp.sum(acc, axis=0, keepdims=True)

        thr = _kth_largest_key(count_ge, n_sel, (1, LANES))
        thr = jnp.maximum(thr, INT_MIN + 1)
        thr_ref[...] = jnp.broadcast_to(thr, thr_ref.shape)


def _page_spec(width, k):
    return pl.BlockSpec((None, PAGE_SIZE, width), lambda b, s, pt: (pt[b, s * PAGES_PER_STEP + k], 0, 0))


def _sample_score(page_table, qi_t, w_t, ki_new, cache_ki, n_sel, n_new):
    nb, n_pages = page_table.shape
    n_steps = n_pages // PAGES_PER_STEP
    per_b = lambda b, s, pt: (b, 0, 0)
    grid_spec = pltpu.PrefetchScalarGridSpec(
        num_scalar_prefetch=1,
        grid=(nb, n_steps),
        in_specs=[
            pl.BlockSpec((None, IDX_DIM, LANES), per_b),
            pl.BlockSpec((None, 1, LANES), per_b),
            pl.BlockSpec((None, PAGE_SIZE, IDX_DIM), per_b),
        ] + [_page_spec(IDX_DIM, k) for k in range(PAGES_PER_STEP)],
        out_specs=[
            pl.BlockSpec((None, PAGES_PER_STEP * PAGE_SIZE, LANES), lambda b, s, pt: (b, s, 0)),
            pl.BlockSpec((None, PAGE_SIZE, LANES), per_b),
            pl.BlockSpec((None, 8, LANES), per_b),
        ],
        scratch_shapes=[pltpu.VMEM((n_pages * PAGE_SIZE + PAGE_SIZE, LANES), I32)],
    )
    return pl.pallas_call(
        functools.partial(_sample_score_kernel, n_sel=n_sel, n_new=n_new),
        grid_spec=grid_spec,
        out_shape=[
            jax.ShapeDtypeStruct((nb, n_pages * PAGE_SIZE, LANES), I32),
            jax.ShapeDtypeStruct((nb, PAGE_SIZE, LANES), I32),
            jax.ShapeDtypeStruct((nb, 8, LANES), I32),
        ],
        compiler_params=_params("parallel", "arbitrary"),
        name="sample_score",
    )(page_table, qi_t, w_t, ki_new, *([cache_ki] * PAGES_PER_STEP))


def _sample_attend_kernel(pt_ref, qbd_ref, keys_ref, keysnew_ref, thr_ref, knew_ref, vnew_ref,
                          diag_ref, *rest):
    kpages = rest[:PAGES_PER_STEP]
    vpages = rest[PAGES_PER_STEP:2 * PAGES_PER_STEP]
    out_ref, m_ref, l_ref, acc_ref = rest[2 * PAGES_PER_STEP:]
    s = pl.program_id(1)
    n_steps = pl.num_programs(1)

    @pl.when(s == 0)
    def _init():
        m_ref[...] = jnp.full(m_ref.shape, NEG_BIG, F32)
        l_ref[...] = jnp.zeros(l_ref.shape, F32)
        acc_ref[...] = jnp.zeros(acc_ref.shape, F32)

    qbd = qbd_ref[...]
    thr = thr_ref[0:1, :]

    def process(kp, vp, keyblk):
        lg = jnp.dot(kp.astype(BF16), qbd, preferred_element_type=F32) * ATTN_SCALE
        lm = jnp.where(keyblk >= thr, lg, NEG_BIG)
        m = m_ref[...]
        m_new = jnp.maximum(m, jnp.max(lm, axis=0, keepdims=True))
        pe = jnp.exp(lm - m_new)
        alpha = jnp.exp(m - m_new)
        l_ref[...] = alpha * l_ref[...] + jnp.sum(pe, axis=0, keepdims=True)
        pv = jnp.dot(vp.T.astype(BF16), pe.astype(BF16), preferred_element_type=F32)
        acc_ref[...] = alpha * acc_ref[...] + pv
        m_ref[...] = m_new

    for i in range(PAGES_PER_STEP):
        process(kpages[i][...], vpages[i][...], keys_ref[i * PAGE_SIZE:(i + 1) * PAGE_SIZE, :])

    @pl.when(s == n_steps - 1)
    def _finish():
        process(knew_ref[...], vnew_ref[...], keysnew_ref[...])
        out_ref[...] = _lane_group_allsum(acc_ref[...] / l_ref[...] * diag_ref[...])


def _sample_attend(page_table, qbd, keys, keys_new, thr, k_new, v_new, diag, cache_k, cache_v):
    nb, n_pages = page_table.shape
    n_steps = n_pages // PAGES_PER_STEP
    per_b = lambda b, s, pt: (b, 0, 0)
    grid_spec = pltpu.PrefetchScalarGridSpec(
        num_scalar_prefetch=1,
        grid=(nb, n_steps),
        in_specs=[
            pl.BlockSpec((None, ATTN_WIDTH, LANES), per_b),
            pl.BlockSpec((None, PAGES_PER_STEP * PAGE_SIZE, LANES), lambda b, s, pt: (b, s, 0)),
            pl.BlockSpec((None, PAGE_SIZE, LANES), per_b),
            pl.BlockSpec((None, 8, LANES), per_b),
            pl.BlockSpec((None, PAGE_SIZE, ATTN_WIDTH), per_b),
            pl.BlockSpec((None, PAGE_SIZE, ATTN_WIDTH), per_b),
            pl.BlockSpec((ATTN_WIDTH, LANES), lambda b, s, pt: (0, 0)),
        ] + [_page_spec(ATTN_WIDTH, k) for k in range(PAGES_PER_STEP)] * 2,
        out_specs=pl.BlockSpec((None, ATTN_WIDTH, LANES), per_b),
        scratch_shapes=[pltpu.VMEM((1, LANES), F32), pltpu.VMEM((1, LANES), F32),
                        pltpu.VMEM((ATTN_WIDTH, LANES), F32)],
    )
    return pl.pallas_call(
        _sample_attend_kernel,
        grid_spec=grid_spec,
        out_shape=jax.ShapeDtypeStruct((nb, ATTN_WIDTH, LANES), F32),
        compiler_params=_params("parallel", "arbitrary"),
        name="sample_attend",
    )(page_table, qbd, keys, keys_new, thr, k_new, v_new, diag,
      *([cache_k] * PAGES_PER_STEP), *([cache_v] * PAGES_PER_STEP))


def _mix_kernel(a_ref, u_ref, gv_ref, x_ref, wsp_ref, bias_ref, wo_ref, g_ref, x1_ref, hp_ref):
    gvb = gv_ref[...].astype(BF16)
    group = lax.broadcasted_iota(I32, gvb.shape, 1) // HEAD_DIM
    mixed = bias_ref[...]
    for g in range(N_GROUPS):
        r = jnp.dot(wsp_ref[g], gvb, preferred_element_type=F32)
        mixed = mixed + jnp.where(group == g, r, 0.0)
    gm = (u_ref[...] * mixed).astype(BF16)
    mix = (jnp.dot(a_ref[...], wo_ref[:ATTN_WIDTH, :], preferred_element_type=F32)
           + jnp.dot(gm, wo_ref[ATTN_WIDTH:, :], preferred_element_type=F32))
    x1 = x_ref[...] + mix
    x1_ref[...] = x1
    hp_ref[...] = _rms(x1, g_ref[...]).astype(BF16)


def _mix(a, u, gv, x, wsp, bias, wo, g):
    n, d = x.shape
    row = lambda i: (i, 0)
    fixed = lambda i: (0, 0)
    return pl.pallas_call(
        _mix_kernel,
        grid=(n // CHUNK,),
        in_specs=[
            pl.BlockSpec((CHUNK, ATTN_WIDTH), row),
            pl.BlockSpec((CHUNK, GMLP_WIDTH), row),
            pl.BlockSpec((CHUNK, GMLP_WIDTH), row),
            pl.BlockSpec((CHUNK, d), row),
            pl.BlockSpec(wsp.shape, lambda i: (0, 0, 0)),
            pl.BlockSpec(bias.shape, fixed),
            pl.BlockSpec(wo.shape, fixed),
            pl.BlockSpec((1, d), fixed),
        ],
        out_specs=[pl.BlockSpec((CHUNK, d), row), pl.BlockSpec((CHUNK, d), row)],
        out_shape=[jax.ShapeDtypeStruct((n, d), F32), jax.ShapeDtypeStruct((n, d), BF16)],
        compiler_params=_params("parallel"),
        name="mix",
    )(a, u, gv, x, wsp, bias, wo, g)


def _col_reduce(x, op):
    slabs = [x[i:i + 8] for i in range(0, x.shape[0], 8)]
    while len(slabs) > 1:
        nxt = [op(slabs[i], slabs[i + 1]) for i in range(0, len(slabs) - 1, 2)]
        if len(slabs) % 2:
            nxt.append(slabs[-1])
        slabs = nxt
    red = jnp.max if op is jnp.maximum else jnp.min
    return red(slabs[0], axis=0, keepdims=True)


def _extract16(arrays):
    rows, n = arrays[0].shape
    ridx = lax.broadcasted_iota(I32, (rows, n), 0).astype(F32)
    r16 = lax.broadcasted_iota(I32, (PEER_TOPK, n), 0)

    def body(r, carry):
        out = []
        for cur, vals, rank in carry:
            mx = _col_reduce(cur, jnp.maximum)
            hit = ridx == _col_reduce(jnp.where(cur == mx, ridx, float(rows)), jnp.minimum)
            out.append((jnp.where(hit, -jnp.inf, cur), jnp.where(r16 == r, mx, vals),
                        jnp.where(hit, r.astype(F32), rank)))
        return tuple(out)

    init = tuple((s, jnp.zeros((PEER_TOPK, n), F32), jnp.full((rows, n), float(PEER_TOPK), F32))
                 for s in arrays)
    res = lax.fori_loop(0, PEER_TOPK, body, init)
    return [(vals, rank) for _, vals, rank in res]


def _batcher_pairs(n):
    pairs = []

    def merge(lo, hi, r):
        step = r * 2
        if step < hi - lo:
            merge(lo, hi, step)
            merge(lo + r, hi, step)
            pairs.extend((i, i + r) for i in range(lo + r, hi - r, step))
        else:
            pairs.append((lo, lo + r))

    def sort(lo, hi):
        if hi - lo >= 1:
            mid = lo + (hi - lo) // 2
            sort(lo, mid)
            sort(mid + 1, hi)
            merge(lo, hi, 1)

    sort(0, n - 1)
    return pairs


_SORT16 = _batcher_pairs(PEER_TOPK)
SUBLANES = 8


def _compare_exchange(v, i, j):
    v[i], v[j] = jnp.maximum(v[i], v[j]), jnp.minimum(v[i], v[j])


def _merge_top16(a, b):
    v = [jnp.maximum(a[i], b[PEER_TOPK - 1 - i]) for i in range(PEER_TOPK)]
    d = PEER_TOPK // 2
    while d:
        for i in range(PEER_TOPK):
            if not i & d:
                _compare_exchange(v, i, i + d)
        d //= 2
    return v


def _fold_sublanes(v):
    for sh in (4, 2, 1):
        v = _merge_top16(v, [pltpu.roll(x, sh, 0) for x in v])
    return v


def _top16_values(x):
    v = [x[i * SUBLANES:(i + 1) * SUBLANES] for i in range(x.shape[0] // SUBLANES)]
    for i, j in _SORT16:
        _compare_exchange(v, i, j)
    return _fold_sublanes(v)


def _rows(slab, n_slabs):
    return jnp.concatenate([slab] * n_slabs, axis=0)


def _peer_gates_fast(s1, s2):
    n_slabs = s1.shape[0] // SUBLANES
    a = _top16_values(s1)
    b = _top16_values(s2)
    sub = lax.broadcasted_iota(I32, a[0].shape, 0)

    def spread(vals, g):
        out = vals[g * SUBLANES + SUBLANES - 1]
        for t in range(SUBLANES - 2, -1, -1):
            out = jnp.where(sub == t, vals[g * SUBLANES + t], out)
        return out

    b_lo, b_hi = spread(b, 0), spread(b, 1)
    cands = [x + b_lo for x in a] + [x + b_hi for x in a]
    best = _fold_sublanes(_merge_top16(cands[:PEER_TOPK], cands[PEER_TOPK:]))
    tau = best[PEER_TOPK - 1]
    z = functools.reduce(jnp.add, [jnp.exp(x - best[0]) for x in best])
    in1 = s1 >= _rows(a[PEER_TOPK - 1], n_slabs)
    in2 = s2 >= _rows(b[PEER_TOPK - 1], n_slabs)
    e1 = jnp.where(in1, jnp.exp(s1 - _rows(a[0], n_slabs)), 0.0)
    e2 = jnp.where(in2, jnp.exp(s2 - _rows(b[0], n_slabs)), 0.0) / _rows(z, n_slabs)
    count = lambda m: jnp.sum(jnp.where(m, 1.0, 0.0), axis=0, keepdims=True)
    n_pairs = count(jnp.concatenate(cands, axis=0) >= _rows(tau, 2 * PEER_TOPK))
    tied = jnp.max(jnp.maximum(jnp.maximum(count(in1), count(in2)), n_pairs)) > PEER_TOPK + 0.5
    return e1, e2, tau, tied


def _peer_gates_exact(s1, s2):
    n = s1.shape[1]
    (a, k1), (b, k2) = _extract16([s1, s2])
    cand = jnp.concatenate([a[r:r + 1, :] + b for r in range(PEER_TOPK)], axis=0)
    (best, kc), = _extract16([cand])
    z = jnp.sum(jnp.exp(best - best[0:1, :]), axis=0, keepdims=True)
    e1 = jnp.where(k1 < PEER_TOPK, jnp.exp(s1 - a[0:1, :]), 0.0)
    e2 = jnp.where(k2 < PEER_TOPK, jnp.exp(s2 - b[0:1, :]), 0.0) / z
    taken = jnp.where(kc < PEER_TOPK, 1.0, 0.0)
    r1 = jnp.zeros_like(s1)
    for r in range(PEER_TOPK):
        row_len = jnp.sum(taken[r * PEER_TOPK:(r + 1) * PEER_TOPK], axis=0, keepdims=True)
        r1 = r1 + jnp.where(k1 == r, row_len, 0.0)
    r2 = jnp.where(k2 < PEER_TOPK, -k2, -4.0 * PEER_TOPK)
    return e1, e2, r1, r2, jnp.full((SUBLANES, n), 0.5, F32)


def _peer_kernel(hp_ref, x1_ref, wpqt_ref, sk_ref, u_ref, vt_ref, gf_ref, y_ref,
                 s1_ref, s2_ref, e1_ref, e2_ref, tau_ref, out_ref, *, tm, eb):
    e = pl.program_id(1)
    n_e = pl.num_programs(1)
    hp = hp_ref[...]
    nt = (((1,), (1,)), ((), ()))

    @pl.when(e == 0)
    def _select():
        qt = lax.dot_general(wpqt_ref[...], hp, nt, preferred_element_type=F32).astype(BF16)
        for hh in range(N_PEER_HEADS):
            for c, ref in enumerate((s1_ref, s2_ref)):
                idx = hh * 2 + c
                ref[hh] = jnp.dot(sk_ref[idx], qt[idx * PEER_HALF:(idx + 1) * PEER_HALF, :],
                                  preferred_element_type=F32)

        def head_body(hh, carry):
            for ch in range(tm // LANES):
                sl = slice(ch * LANES, (ch + 1) * LANES)
                s1 = s1_ref[hh, :, sl]
                s2 = s2_ref[hh, :, sl]
                e1, e2, tau, tied = _peer_gates_fast(s1, s2)
                e1_ref[hh, :, sl] = e1
                e2_ref[hh, :, sl] = e2
                tau_ref[hh, :, sl] = tau

                @pl.when(tied)
                def _redo(hh=hh, sl=sl, s1=s1, s2=s2):
                    e1x, e2x, r1, r2, cut = _peer_gates_exact(s1, s2)
                    e1_ref[hh, :, sl] = e1x
                    e2_ref[hh, :, sl] = e2x
                    s1_ref[hh, :, sl] = r1
                    s2_ref[hh, :, sl] = r2
                    tau_ref[hh, :, sl] = cut
            return carry

        lax.fori_loop(0, N_PEER_HEADS, head_body, 0)
        out_ref[...] = jnp.zeros(out_ref.shape, F32)

    act = _gelu(lax.dot_general(u_ref[...], hp, nt, preferred_element_type=F32))
    parts = []
    for ii in range(eb // N_KEYS):
        i = e * (eb // N_KEYS) + ii
        wt = jnp.zeros((N_KEYS, tm), F32)
        for hh in range(N_PEER_HEADS):
            cs = s1_ref[hh, pl.ds(i, 1), :] + s2_ref[hh]
            wt = wt + jnp.where(cs >= tau_ref[hh, 0:1, :],
                                e1_ref[hh, pl.ds(i, 1), :] * e2_ref[hh], 0.0)
        parts.append((wt * act[ii * N_KEYS:(ii + 1) * N_KEYS, :]).astype(BF16))
    gt = jnp.concatenate(parts, axis=0)
    out_ref[...] += jnp.dot(vt_ref[...], gt, preferred_element_type=F32)

    @pl.when(e == n_e - 1)
    def _finish():
        y_ref[...] = _rms(x1_ref[...] + out_ref[...].T, gf_ref[...])


def _peer(hp, x1, wpqt, sk, eu, evt, gf, tm, eb):
    n, d = x1.shape
    n_exp = eu.shape[0]
    row = lambda t, e: (t, 0)
    fixed = lambda t, e: (0, 0)
    head_scr = pltpu.VMEM((N_PEER_HEADS, N_KEYS, tm), F32)
    return pl.pallas_call(
        functools.partial(_peer_kernel, tm=tm, eb=eb),
        grid=(n // tm, n_exp // eb),
        in_specs=[
            pl.BlockSpec((tm, d), row),
            pl.BlockSpec((tm, d), row),
            pl.BlockSpec(wpqt.shape, fixed),
            pl.BlockSpec(sk.shape, lambda t, e: (0, 0, 0)),
            pl.BlockSpec((eb, d), lambda t, e: (e, 0)),
            pl.BlockSpec((d, eb), lambda t, e: (0, e)),
            pl.BlockSpec((1, d), fixed),
        ],
        out_specs=pl.BlockSpec((tm, d), row),
        out_shape=jax.ShapeDtypeStruct((n, d), F32),
        scratch_shapes=[head_scr, head_scr, head_scr, head_scr,
                        pltpu.VMEM((N_PEER_HEADS, 8, tm), F32), pltpu.VMEM((d, tm), F32)],
        compiler_params=_params("parallel", "arbitrary"),
        name="peer",
    )(hp, x1, wpqt, sk, eu, evt, gf)


def _split_w_in(w_in_l):
    a = 3 * ATTN_WIDTH + N_IDX_HEADS * IDX_DIM
    b = a + IDX_DIM + N_IDX_HEADS
    wm = jnp.concatenate([w_in_l[:, :a], w_in_l[:, b:]], axis=1).astype(BF16)
    ws = jnp.pad(w_in_l[:, a:b], ((0, 0), (0, LANES - (b - a)))).astype(BF16)
    return wm, ws


def _prompt_dsa_operands(q, k, v, qi, small, b, t, tq, tk):
    nq = t // tq
    qi_t = qi.reshape(b, nq, tq, N_IDX_HEADS, IDX_DIM).transpose(0, 1, 4, 3, 2)
    qi_t = qi_t.reshape(b, nq, IDX_DIM, N_IDX_HEADS * tq).astype(BF16)
    wi = small[:, IDX_DIM:IDX_DIM + N_IDX_HEADS]
    w_t = wi.reshape(b, nq, tq, N_IDX_HEADS).transpose(0, 1, 3, 2).reshape(b, nq, 1, N_IDX_HEADS * tq)
    ki = small[:, :IDX_DIM].reshape(b, t, IDX_DIM).astype(BF16)
    qt = q.astype(BF16).reshape(b, nq, tq, N_HEADS // 2, 2, HEAD_DIM).transpose(0, 1, 3, 4, 5, 2)
    eye = jnp.eye(2, dtype=BF16)
    qbd = qt[:, :, :, :, :, None, :] * eye[None, None, None, :, None, :, None]
    qbd = qbd.reshape(b, nq, N_HEADS // 2, 2 * HEAD_DIM, 2 * tq)
    kb = k.astype(BF16).reshape(b, t, ATTN_WIDTH)
    vt = v.astype(BF16).reshape(b, t // tk, tk, ATTN_WIDTH).transpose(0, 1, 3, 2)
    return qi_t, w_t, ki, qbd, kb, vt


def _sample_group_attention(qs, ks, vs, qis, smalls, page_table, cache_k_l, cache_v_l, cache_ki_l, bd, tn):
    past = page_table.shape[1] * PAGE_SIZE
    lane_pad = LANES - N_HEADS * tn
    qi_st = qis.reshape(bd, tn, N_IDX_HEADS, IDX_DIM).transpose(0, 3, 2, 1).reshape(bd, IDX_DIM, N_IDX_HEADS * tn)
    qi_st = jnp.pad(qi_st, ((0, 0), (0, 0), (0, lane_pad))).astype(BF16)
    w_st = smalls[:, IDX_DIM:IDX_DIM + N_IDX_HEADS].reshape(bd, tn, N_IDX_HEADS).transpose(0, 2, 1)
    w_st = jnp.pad(w_st.reshape(bd, 1, N_IDX_HEADS * tn), ((0, 0), (0, 0), (0, lane_pad)))
    row_pad = ((0, 0), (0, PAGE_SIZE - tn), (0, 0))
    ki_new = jnp.pad(smalls[:, :IDX_DIM].reshape(bd, tn, IDX_DIM), row_pad)
    n_sel = min(MAX_TOPK, (past + tn) // 4)
    keys, keys_new, thr = _sample_score(page_table, qi_st, w_st, ki_new, cache_ki_l, n_sel, tn)
    qbd_s = jnp.einsum('bthd,hg->bhdgt', qs.reshape(bd, tn, N_HEADS, HEAD_DIM), jnp.eye(N_HEADS, dtype=F32))
    qbd_s = jnp.pad(qbd_s.reshape(bd, ATTN_WIDTH, N_HEADS * tn), ((0, 0), (0, 0), (0, lane_pad))).astype(BF16)
    k_new = jnp.pad(ks.reshape(bd, tn, ATTN_WIDTH), row_pad)
    v_new = jnp.pad(vs.reshape(bd, tn, ATTN_WIDTH), row_pad)
    diag = ((jnp.arange(ATTN_WIDTH)[:, None] // HEAD_DIM) == (jnp.arange(LANES)[None, :] // tn)).astype(F32)
    n_pool = cache_k_l.shape[0]
    a_st = _sample_attend(page_table, qbd_s, keys, keys_new, thr, k_new, v_new, diag,
                          cache_k_l.reshape(n_pool, PAGE_SIZE, ATTN_WIDTH),
                          cache_v_l.reshape(n_pool, PAGE_SIZE, ATTN_WIDTH))
    return a_st[:, :, :tn].transpose(0, 2, 1).reshape(bd * tn, ATTN_WIDTH)


def _sample_gmlp_weights(w_sp_l, bias_p, tn):
    per_chunk = CHUNK // tn
    tril = jnp.tril(jnp.ones((tn, tn), dtype=bool))
    wsp_s = jnp.einsum('ij,gts->gitjs', jnp.eye(per_chunk, dtype=F32),
                       jnp.where(tril[None], w_sp_l[:, :tn, :tn], 0))
    wsp_s = wsp_s.reshape(N_GROUPS, CHUNK, CHUNK).astype(BF16)
    return wsp_s, jnp.tile(bias_p[:tn], (per_chunk, 1))


def kernel(x_prompt, x_sample, cache_k, cache_v, cache_kidx, page_table, norm_mix_g, w_in,
           gv_norm_g, w_sp, b_sp, w_out, norm_ffn_g, w_pq, peer_sub_keys, expert_u, expert_v,
           norm_final_g):
    b, t, d = x_prompt.shape
    bd, tn, _ = x_sample.shape
    depth = w_in.shape[0]
    assert depth == 1 and tn == 8 and t % 256 == 0 and (bd * tn) % CHUNK == 0
    n_pages = page_table.shape[1]
    past = n_pages * PAGE_SIZE
    l = 0

    wm, ws = _split_w_in(w_in[l])
    g_mix = norm_mix_g[l].reshape(1, d)
    gvg = gv_norm_g[l].reshape(1, GMLP_WIDTH)
    lane_group = jnp.arange(GMLP_WIDTH) // HEAD_DIM
    gmean = (lane_group[:, None] == lane_group[None, :]).astype(BF16) * (1.0 / HEAD_DIM)
    invf = ROPE_THETA ** (-jnp.arange(ROPE_HALF, dtype=F32) / ROPE_HALF)
    invf = jnp.tile(invf, LANES // ROPE_HALF).reshape(1, LANES)
    tril = jnp.tril(jnp.ones((CHUNK, CHUNK), dtype=bool))
    wsp_p = jnp.where(tril[None], w_sp[l], 0).astype(BF16)
    bias_p = jnp.repeat(b_sp[l].T, HEAD_DIM, axis=1)
    wsp_s, bias_s = _sample_gmlp_weights(w_sp[l], bias_p, tn)
    wo =w_out[l].astype(BF16)
    g_ffn = norm_ffn_g[l].reshape(1, d)
    wpqt = w_pq[l].T.astype(BF16)
    sk = peer_sub_keys[l].reshape(2 * N_PEER_HEADS, N_KEYS, PEER_HALF).astype(BF16)
    eu = expert_u[l].astype(BF16)
    evt = expert_v[l].T.astype(BF16)
    gf = norm_final_g.reshape(1, d)

    n_p = b * t
    pos_p = jnp.tile(jnp.arange(t, dtype=F32), b).reshape(n_p, 1)
    q, k, v, qi, u, gv, small = _project(x_prompt.reshape(n_p, d), pos_p, invf, g_mix, wm, ws,
                                         gvg, gmean, 256)
    tq, tk = 256, 256
    ops = _prompt_dsa_operands(q, k, v, qi, small, b, t, tq, tk)
    a_t = _dsa_prompt(*ops, tq, tk, min(MAX_TOPK, t // 4))
    a_p = a_t.transpose(0, 2, 1).reshape(n_p, ATTN_WIDTH).astype(BF16)
    x1_p, hp_p = _mix(a_p, u, gv, x_prompt.reshape(n_p, d), wsp_p, bias_p, wo, g_ffn)

    n_s = bd * tn
    pos_s = jnp.tile(past + jnp.arange(tn, dtype=F32), bd).reshape(n_s, 1)
    qs, ks, vs, qis, us, gvs, smalls = _project(x_sample.reshape(n_s, d), pos_s, invf, g_mix, wm,
                                                ws, gvg, gmean, n_s)
    a_s = _sample_group_attention(qs, ks, vs, qis, smalls, page_table, cache_k[l], cache_v[l],
                                  cache_kidx[l], bd, tn).astype(BF16)
    x1_s, hp_s = _mix(a_s, us, gvs, x_sample.reshape(n_s, d), wsp_s, bias_s, wo, g_ffn)

    x1 = jnp.concatenate([x1_p, x1_s], axis=0)
    hp = jnp.concatenate([hp_p, hp_s], axis=0)
    n_all = n_p + n_s
    tm = 640 if n_all % 640 == 0 else 128
    y = _peer(hp, x1, wpqt, sk, eu, evt, gf, tm, 1024)

    heads = (N_HEADS, HEAD_DIM)
    return (y[:n_p].reshape(b, t, d),
            y[n_p:].reshape(bd, tn, d),
            k.reshape(1, b, t, *heads),
            v.reshape(1, b, t, *heads),
            small[:, :IDX_DIM].reshape(1, b, t, IDX_DIM),
            ks.reshape(1, bd, tn, *heads),
            vs.reshape(1, bd, tn, *heads),
            smalls[:, :IDX_DIM].reshape(1, bd, tn, IDX_DIM),
            gvs.reshape(1, bd, tn, N_GROUPS, HEAD_DIM))
```

```python
import functools

import jax
import jax.numpy as jnp
from jax import lax
from jax.experimental import pallas as pl
from jax.experimental.pallas import tpu as pltpu

F32 = jnp.float32
BF16 = jnp.bfloat16
I32 = jnp.int32

EPS = 1e-6
ROPE_THETA = 10000.0
HEAD_DIM = 64
ROPE_HALF = HEAD_DIM // 2
N_HEADS = 8
ATTN_WIDTH = N_HEADS * HEAD_DIM
N_IDX_HEADS = 8
IDX_DIM = 64
MAX_TOPK = 256
N_GROUPS = 8
GMLP_WIDTH = N_GROUPS * HEAD_DIM
CHUNK = 128
PAGE_SIZE = 128
N_PEER_HEADS = 8
N_KEYS = 128
PEER_HALF = 64
PEER_TOPK = 16
LANES = 128

INT_MIN = -(2 ** 31)
NEG_BIG = -1e30
IDX_SCALE = (IDX_DIM ** -0.5) * (N_IDX_HEADS ** -0.5)
ATTN_SCALE = HEAD_DIM ** -0.5
LOG2E = 1.4426950408889634

VMEM_LIMIT_BYTES = 56 * 1024 * 1024


def _params(*sem):
    return pltpu.CompilerParams(dimension_semantics=sem, vmem_limit_bytes=VMEM_LIMIT_BYTES)


def _gelu(x):
    cdf = 0.5 * (1.0 + jnp.tanh(0.7978845608028654 * (x + 0.044715 * (x * x * x))))
    return x * cdf


def _rms(x, g):
    return x * lax.rsqrt(jnp.mean(x * x, axis=-1, keepdims=True) + EPS) * g


def _split3(x):
    a = x.astype(BF16)
    r = x - a.astype(F32)
    b = r.astype(BF16)
    c = (r - b.astype(F32)).astype(BF16)
    return a, b, c


def _dot_f32_by_bf16(x, m):
    a, b, c = _split3(x)
    d = lambda p: jnp.dot(p, m, preferred_element_type=F32)
    return d(a) + d(b) + d(c)


def _code_to_f32(code):
    bits = jnp.where(code < 0, code ^ 0x7FFFFFFF, code)
    return lax.bitcast_convert_type(bits, F32)


LOWEST_FINITE_CODE = -(2 ** 31) + 0x00800000


COUNT_ROWS = 64


def _count_partial(mask):
    rows, n = mask.shape
    ones = jnp.where(mask, 1.0, 0.0)
    return jnp.sum(ones.reshape(rows // COUNT_ROWS, COUNT_ROWS, n), axis=0)


def _kth_largest_key(count_ge, k, shape):
    c0 = count_ge(jnp.zeros(shape, I32))
    t0 = jnp.where(c0 >= k, 0, INT_MIN).astype(I32)

    def body(i, t):
        cand = t + lax.shift_left(jnp.int32(1), 30 - i)
        return jnp.where(count_ge(cand) >= k, cand, t)

    return lax.fori_loop(0, 31, body, t0)


def _project_kernel(x_ref, pos_ref, invf_ref, g_ref, wm_ref, ws_ref, gvg_ref, gmean_ref,
                    q_ref, k_ref, v_ref, qi_ref, u_ref, gv_ref, small_ref):
    h = _rms(x_ref[...], g_ref[...]).astype(BF16)
    ang = pos_ref[...] * invf_ref[...]
    cos = jnp.cos(ang)
    sin = jnp.sin(ang)
    lane = lax.broadcasted_iota(I32, ang.shape, 1)
    first = (lane % HEAD_DIM) < ROPE_HALF
    sin_signed = jnp.where(first, -sin, sin)

    def rope(xc):
        partner = jnp.where(first, pltpu.roll(xc, LANES - ROPE_HALF, 1), pltpu.roll(xc, ROPE_HALF, 1))
        return xc * cos + partner * sin_signed

    def proj(c0):
        return jnp.dot(h, wm_ref[:, c0:c0 + ATTN_WIDTH], preferred_element_type=F32)

    for ref, c0 in ((q_ref, 0), (k_ref, ATTN_WIDTH), (qi_ref, 3 * ATTN_WIDTH)):
        p = proj(c0)
        for c in range(ATTN_WIDTH // LANES):
            ref[:, c * LANES:(c + 1) * LANES] = rope(p[:, c * LANES:(c + 1) * LANES])
    v_ref[...] = proj(2 * ATTN_WIDTH)
    u_ref[...] = _gelu(proj(4 * ATTN_WIDTH))
    gl = _gelu(proj(5 * ATTN_WIDTH))
    ms = _dot_f32_by_bf16(gl * gl, gmean_ref[...])
    gv_ref[...] = gl * lax.rsqrt(ms + EPS) * gvg_ref[...]
    sm = jnp.dot(h, ws_ref[...], preferred_element_type=F32)
    small_ref[...] = jnp.where(lane < IDX_DIM, rope(sm), sm)


def _project(x2d, pos, invf, g, wm, ws, gvg, gmean, tm):
    n, d = x2d.shape
    row = lambda i: (i, 0)
    fixed = lambda i: (0, 0)
    wide = jax.ShapeDtypeStruct((n, ATTN_WIDTH), F32)
    return pl.pallas_call(
        _project_kernel,
        grid=(n // tm,),
        in_specs=[
            pl.BlockSpec((tm, d), row),
            pl.BlockSpec((tm, 1), row),
            pl.BlockSpec((1, LANES), fixed),
            pl.BlockSpec((1, d), fixed),
            pl.BlockSpec(wm.shape, fixed),
            pl.BlockSpec(ws.shape, fixed),
            pl.BlockSpec((1, GMLP_WIDTH), fixed),
            pl.BlockSpec(gmean.shape, fixed),
        ],
        out_specs=[pl.BlockSpec((tm, ATTN_WIDTH), row)] * 6 + [pl.BlockSpec((tm, LANES), row)],
        out_shape=[wide] * 6 + [jax.ShapeDtypeStruct((n, LANES), F32)],
        compiler_params=_params("parallel"),
        name="project",
    )(x2d, pos, invf, g, wm, ws, gvg, gmean)


def _dsa_prompt_kernel(qi_ref, w_ref, ki_ref, qbd_ref, k_ref, vt_ref, out_ref, keys_ref, acc_ref, lg_ref,
                       *, tq, tk, n_sel):
    j = pl.program_id(1)
    n_tiles = pl.num_programs(1)
    sub = tq // tk
    nkb = (j + 1) * sub
    q0 = j * tq
    qi = qi_ref[0, 0]
    w = w_ref[0, 0] * IDX_SCALE
    qpos = q0 + lax.broadcasted_iota(I32, (tk, tq), 1)
    krow = lax.broadcasted_iota(I32, (tk, tq), 0)

    def score_body(kq, carry):
        for u in range(sub):
            r0 = pl.multiple_of(kq * tq, tq) + u * tk
            d = jnp.dot(ki_ref[0, pl.ds(r0, tk), :], qi, preferred_element_type=F32)
            s = jnp.zeros((tk, tq), F32)
            for h in range(N_IDX_HEADS):
                s = s + jnp.maximum(d[:, h * tq:(h + 1) * tq], 0.0) * w[:, h * tq:(h + 1) * tq]
            keys_ref[pl.ds(r0, tk), :] = jnp.where(krow + r0 <= qpos, s, -jnp.inf)
        return carry

    lax.fori_loop(0, j + 1, score_body, 0)

    @pl.when(jnp.logical_and((j + 1) % 2 == 1, j + 1 < n_tiles))
    def _blank():
        keys_ref[pl.ds(pl.multiple_of((j + 1) * tq, tq), tq), :] = jnp.full((tq, tq), -jnp.inf, F32)

    def count_ge(code):
        cand = _code_to_f32(code)

        def body(kb, acc):
            blk = keys_ref[pl.ds(pl.multiple_of(kb * 2 * tq, 2 * tq), 2 * tq), :]
            return acc + _count_partial(blk >= cand)
        acc = lax.fori_loop(0, (j + 2) // 2, body, jnp.zeros((COUNT_ROWS, tq), F32))
        return jnp.sum(acc, axis=0, keepdims=True)

    cut = jnp.maximum(_kth_largest_key(count_ge, n_sel, (1, tq)), LOWEST_FINITE_CODE)
    thr = _code_to_f32(cut)

    @pl.when(jnp.max(count_ge(cut)) > n_sel + 0.5)
    def _break_ties():
        blk_rows = 2 * tq
        need = n_sel - count_ge(cut + 1)
        earlier = (lax.broadcasted_iota(I32, (blk_rows, blk_rows), 1)
                   < lax.broadcasted_iota(I32, (blk_rows, blk_rows), 0))
        earlier = jnp.where(earlier, 1.0, 0.0).astype(BF16)

        def body(kb, seen):
            rows = pl.ds(pl.multiple_of(kb * blk_rows, blk_rows), blk_rows)
            blk = keys_ref[rows, :]
            eq = jnp.where(blk == thr, 1.0, 0.0)
            before = jnp.dot(earlier, eq.astype(BF16), preferred_element_type=F32) + seen
            keys_ref[rows, :] = jnp.where(eq * before >= jnp.maximum(need, 0.5), -jnp.inf, blk)
            return seen + jnp.sum(eq, axis=0, keepdims=True)

        lax.fori_loop(0, (j + 2) // 2, body, jnp.zeros((1, tq), F32))

    n_pairs = N_HEADS // 2
    acc_ref[...] = jnp.zeros(acc_ref.shape, F32)

    def logits(kb):
        r0 = pl.multiple_of(kb * tk, tk)
        return [jnp.dot(k_ref[0, pl.ds(r0, tk), p * LANES:(p + 1) * LANES], qbd_ref[0, 0, p],
                        preferred_element_type=F32) for p in range(n_pairs)]

    def stash(slot, lgs):
        for p in range(n_pairs):
            lg_ref[slot, :, p * 2 * tq:(p + 1) * 2 * tq] = lgs[p]

    stash(0, logits(0))

    def att_body(kb, carry):
        ms, ls = carry
        r0 = pl.multiple_of(kb * tk, tk)
        slot = kb % 2
        nxt = logits(jnp.minimum(kb + 1, nkb - 1))
        sel = keys_ref[pl.ds(r0, tk), :] >= thr
        new_ms, new_ls = [], []
        for p in range(n_pairs):
            rows = slice(p * LANES, (p + 1) * LANES)
            lg = lg_ref[slot, :, p * 2 * tq:(p + 1) * 2 * tq] * (ATTN_SCALE * LOG2E)
            lm = jnp.concatenate([jnp.where(sel, lg[:, :tq], NEG_BIG),
                                  jnp.where(sel, lg[:, tq:], NEG_BIG)], axis=1)
            m_new = jnp.maximum(ms[p], jnp.max(lm, axis=0, keepdims=True))
            pe = jnp.exp2(lm - m_new)
            alpha = jnp.exp2(ms[p] - m_new)
            new_ls.append(alpha * ls[p] + jnp.sum(pe, axis=0, keepdims=True))
            new_ms.append(m_new)
            pb = pe.astype(BF16)
            vtb = vt_ref[0, kb, rows, :]
            pv0 = jnp.dot(vtb[:HEAD_DIM], pb[:, :tq], preferred_element_type=F32)
            pv1 = jnp.dot(vtb[HEAD_DIM:], pb[:, tq:], preferred_element_type=F32)
            acc = acc_ref[rows, :]
            acc_ref[rows, :] = jnp.concatenate([alpha[:, :tq] * acc[:HEAD_DIM] + pv0,
                                                alpha[:, tq:] * acc[HEAD_DIM:] + pv1], axis=0)
        stash(1 - slot, nxt)
        return tuple(new_ms), tuple(new_ls)

    init = (tuple(jnp.full((1, 2 * tq), NEG_BIG, F32) for _ in range(n_pairs)),
            tuple(jnp.zeros((1, 2 * tq), F32) for _ in range(n_pairs)))
    _, ls = lax.fori_loop(0, nkb, att_body, init)
    for p in range(n_pairs):
        rows = slice(p * LANES, (p + 1) * LANES)
        acc = acc_ref[rows, :]
        out_ref[0, rows, :] = jnp.concatenate(
            [acc[:HEAD_DIM] / ls[p][:, :tq], acc[HEAD_DIM:] / ls[p][:, tq:]], axis=0)


def _dsa_prompt(qi_t, w_t, ki, qbd, k, vt, tq, tk, n_sel):
    b, nq = qi_t.shape[:2]
    t = k.shape[1]
    tile = lambda bb, j: (bb, j, 0, 0)
    whole3 = lambda bb, j: (bb, 0, 0)
    return pl.pallas_call(
        functools.partial(_dsa_prompt_kernel, tq=tq, tk=tk, n_sel=n_sel),
        grid=(b, nq),
        in_specs=[
            pl.BlockSpec((1, 1, IDX_DIM, N_IDX_HEADS * tq), tile),
            pl.BlockSpec((1, 1, 1, N_IDX_HEADS * tq), tile),
            pl.BlockSpec((1, t, IDX_DIM), whole3, pipeline_mode=pl.Buffered(1)),
            pl.BlockSpec((1, 1, N_HEADS // 2, LANES, 2 * tq), lambda bb, j: (bb, j, 0, 0, 0)),
            pl.BlockSpec((1, t, ATTN_WIDTH), whole3, pipeline_mode=pl.Buffered(1)),
            pl.BlockSpec((1, t // tk, ATTN_WIDTH, tk), lambda bb, j: (bb, 0, 0, 0),
                         pipeline_mode=pl.Buffered(1)),
        ],
        out_specs=pl.BlockSpec((1, ATTN_WIDTH, tq), lambda bb, j: (bb, 0, j)),
        out_shape=jax.ShapeDtypeStruct((b, ATTN_WIDTH, t), F32),
        scratch_shapes=[pltpu.VMEM((t, tq), F32), pltpu.VMEM((ATTN_WIDTH, tq), F32),
                        pltpu.VMEM((2, tk, N_HEADS * tq), F32)],
        compiler_params=_params("parallel", "parallel"),
        name="dsa_prompt",
    )(qi_t, w_t, ki, qbd, k, vt)


PAGES_PER_STEP = 8
STEP_KEYS = PAGES_PER_STEP * PAGE_SIZE


def _sample_score_kernel(pt_ref, qi_ref, w_ref, kinew_ref, *rest, n_sel, n_new):
    pages = rest[:PAGES_PER_STEP]
    keys_ref, keysnew_ref, thr_ref, need_ref, scr = rest[PAGES_PER_STEP:]
    s = pl.program_id(1)
    n_steps = pl.num_programs(1)
    qi = qi_ref[...]
    w = w_ref[...] * IDX_SCALE
    nt = (((1,), (1,)), ((), ()))

    def score(kib):
        d = lax.dot_general(qi, kib.astype(BF16), nt, preferred_element_type=F32)
        sc = jnp.maximum(d, 0.0) * w
        tot = jnp.zeros((n_new, PAGE_SIZE), F32)
        for h in range(N_IDX_HEADS):
            tot = tot + sc[h * n_new:(h + 1) * n_new]
        return tot

    for i, page in enumerate(pages):
        key = score(page[...])
        keys_ref[:, i * PAGE_SIZE:(i + 1) * PAGE_SIZE] = key
        scr[s, :, i * PAGE_SIZE:(i + 1) * PAGE_SIZE] = key

    @pl.when(s == n_steps - 1)
    def _finish():
        key = score(kinew_ref[...])
        qrow = lax.broadcasted_iota(I32, key.shape, 0)
        kcol = lax.broadcasted_iota(I32, key.shape, 1)
        key = jnp.where(kcol <= qrow, key, -jnp.inf)
        keysnew_ref[...] = key
        scr[n_steps] = jnp.concatenate(
            [key, jnp.full((n_new, STEP_KEYS - PAGE_SIZE), -jnp.inf, F32)], axis=1)

        def count_ge(code):
            cand = _code_to_f32(code)
            acc = jnp.zeros((n_new, STEP_KEYS), F32)
            for slab in range(scr.shape[0]):
                acc = acc + jnp.where(scr[slab] >= cand, 1.0, 0.0)
            return jnp.sum(acc, axis=1, keepdims=True)

        cut = jnp.maximum(_kth_largest_key(count_ge, n_sel, (n_new, 1)), LOWEST_FINITE_CODE)
        thr_ref[...] = jnp.broadcast_to(_code_to_f32(cut), thr_ref.shape)
        need = jnp.broadcast_to(n_sel - count_ge(cut + 1), need_ref.shape)
        surplus = jnp.broadcast_to(count_ge(cut) - n_sel, need_ref.shape)
        left = lax.broadcasted_iota(I32, need_ref.shape, 1) < LANES // 2
        need_ref[...] = jnp.where(left, need, surplus)


def _page_spec(tail, k):
    zeros = (0,) * len(tail)
    return pl.BlockSpec((None, None) + tail, lambda b, s, pt, *_: (0, pt[b, s * PAGES_PER_STEP + k]) + zeros)


def _sample_score(page_table, qi_rows, w_col, ki_new, cache_ki, n_sel, n_new):
    nb, n_pages = page_table.shape
    n_steps = n_pages // PAGES_PER_STEP
    per_b = lambda b, s, pt: (b, 0, 0)
    grid_spec = pltpu.PrefetchScalarGridSpec(
        num_scalar_prefetch=1,
        grid=(nb, n_steps),
        in_specs=[
            pl.BlockSpec((None, LANES, IDX_DIM), per_b),
            pl.BlockSpec((None, LANES, 1), per_b),
            pl.BlockSpec((None, PAGE_SIZE, IDX_DIM), per_b),
        ] + [_page_spec((PAGE_SIZE, IDX_DIM), k) for k in range(PAGES_PER_STEP)],
        out_specs=[
            pl.BlockSpec((None, n_new, STEP_KEYS), lambda b, s, pt: (b, 0, s)),
            pl.BlockSpec((None, n_new, PAGE_SIZE), per_b),
            pl.BlockSpec((None, n_new, LANES), per_b),
            pl.BlockSpec((None, n_new, LANES), per_b),
        ],
        scratch_shapes=[pltpu.VMEM((n_steps + 1, n_new, STEP_KEYS), F32)],
    )
    return pl.pallas_call(
        functools.partial(_sample_score_kernel, n_sel=n_sel, n_new=n_new),
        grid_spec=grid_spec,
        out_shape=[
            jax.ShapeDtypeStruct((nb, n_new, n_pages * PAGE_SIZE), F32),
            jax.ShapeDtypeStruct((nb, n_new, PAGE_SIZE), F32),
            jax.ShapeDtypeStruct((nb, n_new, LANES), F32),
            jax.ShapeDtypeStruct((nb, n_new, LANES), F32),
        ],
        compiler_params=_params("parallel", "arbitrary"),
        name="sample_score",
    )(page_table, qi_rows, w_col, ki_new, *([cache_ki] * PAGES_PER_STEP))


def _sample_attend_kernel(pt_ref, tied_ref, q_ref, keys_ref, keysnew_ref, thr_ref, need_ref, knew_ref, vnew_ref,
                          spread_ref, bias_ref, earlier_ref, *rest):
    kpages = rest[:PAGES_PER_STEP]
    vpages = rest[PAGES_PER_STEP:2 * PAGES_PER_STEP]
    out_ref, m_ref, l_ref, acc_ref, seen_ref, lm_ref = rest[2 * PAGES_PER_STEP:]
    s = pl.program_id(1)
    n_steps = pl.num_programs(1)
    n_new = keys_ref.shape[0]
    nt = (((1,), (1,)), ((), ()))

    @pl.when(s == 0)
    def _init():
        m_ref[...] = jnp.full(m_ref.shape, NEG_BIG, F32)
        l_ref[...] = jnp.zeros(l_ref.shape, F32)
        acc_ref[...] = jnp.zeros(acc_ref.shape, F32)
        seen_ref[...] = jnp.zeros(seen_ref.shape, F32)

    q = q_ref[...]
    thr = thr_ref[:, 0:1]
    need = need_ref[:, 0:1]

    tied = tied_ref[pl.program_id(0)] != 0

    def selected_in_order(keyblk):
        eq = jnp.where(keyblk == thr, 1.0, 0.0)
        seen = seen_ref[...]
        before = jnp.dot(eq.astype(BF16), earlier_ref[...], preferred_element_type=F32) + seen
        seen_ref[...] = seen + jnp.sum(eq, axis=1, keepdims=True)
        return jnp.where(keyblk > thr, 1.0, jnp.where(before < need, eq, 0.0))

    def selected(keys):
        def in_order(kk):
            return jnp.concatenate([selected_in_order(kk[:, i:i + PAGE_SIZE])
                                    for i in range(0, kk.shape[1], PAGE_SIZE)], axis=1)
        return lax.cond(tied, in_order, lambda kk: jnp.where(kk >= thr, 1.0, 0.0), keys)

    def masked_logits(kf, chosen):
        lg = lax.dot_general(q, kf.astype(BF16), nt, preferred_element_type=F32)
        lg = lg * (ATTN_SCALE * LOG2E) + bias_ref[...]
        chosen = chosen.astype(BF16)
        chosen = jnp.concatenate([chosen] * (LANES // n_new), axis=0)
        spread = jnp.dot(chosen, spread_ref[...], preferred_element_type=F32)
        return jnp.where(spread > 0.5, lg, NEG_BIG)

    def accumulate(lms, vfs):
        m = m_ref[...]
        m_new = m
        for lm in lms:
            m_new = jnp.maximum(m_new, jnp.max(lm(), axis=1, keepdims=True))
        alpha = jnp.exp2(m - m_new)
        l = alpha * l_ref[...]
        acc = alpha * acc_ref[...]
        for lm, vf in zip(lms, vfs):
            pe = jnp.exp2(lm() - m_new)
            l = l + jnp.sum(pe, axis=1, keepdims=True)
            acc = acc + jnp.dot(pe.astype(BF16), vf().astype(BF16), preferred_element_type=F32)
        m_ref[...] = m_new
        l_ref[...] = l
        acc_ref[...] = acc

    flat = (PAGE_SIZE * N_HEADS, HEAD_DIM)
    chosen = selected(keys_ref[...])
    for i in range(PAGES_PER_STEP):
        lm_ref[i] = masked_logits(kpages[i][...].reshape(flat), chosen[:, i * PAGE_SIZE:(i + 1) * PAGE_SIZE])
    accumulate([functools.partial(lambda i: lm_ref[i], i) for i in range(PAGES_PER_STEP)],
               [functools.partial(lambda i: vpages[i][...].reshape(flat), i) for i in range(PAGES_PER_STEP)])

    @pl.when(s == n_steps - 1)
    def _finish():
        lm_ref[0] = masked_logits(knew_ref[...], selected(keysnew_ref[...]))
        accumulate([lambda: lm_ref[0]], [lambda: vnew_ref[...]])
        out_ref[...] = acc_ref[...] / l_ref[...]


def _sample_attend(page_table, tied, q_rows, keys, keys_new, thr, need, k_new, v_new, spread, bias, earlier,
                   cache_k, cache_v):
    nb, n_pages = page_table.shape
    n_steps = n_pages // PAGES_PER_STEP
    n_new = keys.shape[1]
    per_b = lambda b, s, pt, tied: (b, 0, 0)
    fixed = lambda b, s, pt, tied: (0, 0)
    page = (PAGE_SIZE, N_HEADS, HEAD_DIM)
    grid_spec = pltpu.PrefetchScalarGridSpec(
        num_scalar_prefetch=2,
        grid=(nb, n_steps),
        in_specs=[
            pl.BlockSpec((None, LANES, HEAD_DIM), per_b),
            pl.BlockSpec((None, n_new, STEP_KEYS), lambda b, s, pt, tied: (b, 0, s)),
            pl.BlockSpec((None, n_new, PAGE_SIZE), per_b),
            pl.BlockSpec((None, n_new, LANES), per_b),
            pl.BlockSpec((None, n_new, LANES), per_b),
            pl.BlockSpec((None,) + k_new.shape[1:], per_b),
            pl.BlockSpec((None,) + v_new.shape[1:], per_b),
            pl.BlockSpec(spread.shape, fixed),
            pl.BlockSpec(bias.shape, fixed),
            pl.BlockSpec(earlier.shape, fixed),
        ] + [_page_spec(page, k) for k in range(PAGES_PER_STEP)] * 2,
        out_specs=pl.BlockSpec((None, LANES, HEAD_DIM), per_b),
        scratch_shapes=[pltpu.VMEM((LANES, 1), F32), pltpu.VMEM((LANES, 1), F32),
                        pltpu.VMEM((LANES, HEAD_DIM), F32), pltpu.VMEM((n_new, 1), F32),
                        pltpu.VMEM((PAGES_PER_STEP, LANES, PAGE_SIZE * N_HEADS), F32)],
    )
    return pl.pallas_call(
        _sample_attend_kernel,
        grid_spec=grid_spec,
        out_shape=jax.ShapeDtypeStruct((nb, LANES, HEAD_DIM), F32),
        compiler_params=_params("parallel", "arbitrary"),
        name="sample_attend",
    )(page_table, tied, q_rows, keys, keys_new, thr, need, k_new, v_new, spread, bias, earlier,
      *([cache_k] * PAGES_PER_STEP), *([cache_v] * PAGES_PER_STEP))


def _mix_kernel(a_ref, u_ref, gv_ref, x_ref, wsp_ref, bias_ref, wo_ref, g_ref, x1_ref, hp_ref):
    gvb = gv_ref[...].astype(BF16)
    group = lax.broadcasted_iota(I32, gvb.shape, 1) // HEAD_DIM
    mixed = bias_ref[...]
    for g in range(N_GROUPS):
        r = jnp.dot(wsp_ref[g], gvb, preferred_element_type=F32)
        mixed = mixed + jnp.where(group == g, r, 0.0)
    gm = (u_ref[...] * mixed).astype(BF16)
    mix = (jnp.dot(a_ref[...], wo_ref[:ATTN_WIDTH, :], preferred_element_type=F32)
           + jnp.dot(gm, wo_ref[ATTN_WIDTH:, :], preferred_element_type=F32))
    x1 = x_ref[...] + mix
    x1_ref[...] = x1
    hp_ref[...] = _rms(x1, g_ref[...]).astype(BF16)


def _mix(a, u, gv, x, wsp, bias, wo, g):
    n, d = x.shape
    row = lambda i: (i, 0)
    fixed = lambda i: (0, 0)
    return pl.pallas_call(
        _mix_kernel,
        grid=(n // CHUNK,),
        in_specs=[
            pl.BlockSpec((CHUNK, ATTN_WIDTH), row),
            pl.BlockSpec((CHUNK, GMLP_WIDTH), row),
            pl.BlockSpec((CHUNK, GMLP_WIDTH), row),
            pl.BlockSpec((CHUNK, d), row),
            pl.BlockSpec(wsp.shape, lambda i: (0, 0, 0)),
            pl.BlockSpec(bias.shape, fixed),
            pl.BlockSpec(wo.shape, fixed),
            pl.BlockSpec((1, d), fixed),
        ],
        out_specs=[pl.BlockSpec((CHUNK, d), row), pl.BlockSpec((CHUNK, d), row)],
        out_shape=[jax.ShapeDtypeStruct((n, d), F32), jax.ShapeDtypeStruct((n, d), BF16)],
        compiler_params=_params("parallel"),
        name="mix",
    )(a, u, gv, x, wsp, bias, wo, g)


def _col_reduce(x, op):
    slabs = [x[i:i + 8] for i in range(0, x.shape[0], 8)]
    while len(slabs) > 1:
        nxt = [op(slabs[i], slabs[i + 1]) for i in range(0, len(slabs) - 1, 2)]
        if len(slabs) % 2:
            nxt.append(slabs[-1])
        slabs = nxt
    red = jnp.max if op is jnp.maximum else jnp.min
    return red(slabs[0], axis=0, keepdims=True)


def _extract16(arrays):
    rows, n = arrays[0].shape
    ridx = lax.broadcasted_iota(I32, (rows, n), 0).astype(F32)
    r16 = lax.broadcasted_iota(I32, (PEER_TOPK, n), 0)

    def body(r, carry):
        out = []
        for cur, vals, rank in carry:
            mx = _col_reduce(cur, jnp.maximum)
            hit = ridx == _col_reduce(jnp.where(cur == mx, ridx, float(rows)), jnp.minimum)
            out.append((jnp.where(hit, -jnp.inf, cur), jnp.where(r16 == r, mx, vals),
                        jnp.where(hit, lax.convert_element_type(r, F32), rank)))
        return tuple(out)

    init = tuple((s, jnp.zeros((PEER_TOPK, n), F32), jnp.full((rows, n), float(PEER_TOPK), F32))
                 for s in arrays)
    res = lax.fori_loop(0, PEER_TOPK, body, init)
    return [(vals, rank) for _, vals, rank in res]


def _batcher_pairs(n):
    pairs = []

    def merge(lo, hi, r):
        step = r * 2
        if step < hi - lo:
            merge(lo, hi, step)
            merge(lo + r, hi, step)
            pairs.extend((i, i + r) for i in range(lo + r, hi - r, step))
        else:
            pairs.append((lo, lo + r))

    def sort(lo, hi):
        if hi - lo >= 1:
            mid = lo + (hi - lo) // 2
            sort(lo, mid)
            sort(mid + 1, hi)
            merge(lo, hi, 1)

    sort(0, n - 1)
    return pairs


_SORT16 = _batcher_pairs(PEER_TOPK)
SUBLANES = 8


def _compare_exchange(v, i, j):
    v[i], v[j] = jnp.maximum(v[i], v[j]), jnp.minimum(v[i], v[j])


def _merge_top16(a, b):
    v = [jnp.maximum(a[i], b[PEER_TOPK - 1 - i]) for i in range(PEER_TOPK)]
    d = PEER_TOPK // 2
    while d:
        for i in range(PEER_TOPK):
            if not i & d:
                _compare_exchange(v, i, i + d)
        d //= 2
    return v


def _fold_sublanes(v):
    for sh in (4, 2, 1):
        v = _merge_top16(v, [pltpu.roll(x, sh, 0) for x in v])
    return v


def _top16_values(x):
    v = [x[i * SUBLANES:(i + 1) * SUBLANES] for i in range(x.shape[0] // SUBLANES)]
    for i, j in _SORT16:
        _compare_exchange(v, i, j)
    return _fold_sublanes(v)


def _rows(slab, n_slabs):
    return jnp.concatenate([slab] * n_slabs, axis=0)


def _peer_gates_fast(s1, s2):
    n_slabs = s1.shape[0] // SUBLANES
    a = _top16_values(s1)
    b = _top16_values(s2)
    sub = lax.broadcasted_iota(I32, a[0].shape, 0)

    def spread(vals, g):
        out = vals[g * SUBLANES + SUBLANES - 1]
        for t in range(SUBLANES - 2, -1, -1):
            out = jnp.where(sub == t, vals[g * SUBLANES + t], out)
        return out

    b_lo, b_hi = spread(b, 0), spread(b, 1)
    cands = [x + b_lo for x in a] + [x + b_hi for x in a]
    best = _fold_sublanes(_merge_top16(cands[:PEER_TOPK], cands[PEER_TOPK:]))
    tau = best[PEER_TOPK - 1]
    z = functools.reduce(jnp.add, [jnp.exp(x - best[0]) for x in best])
    in1 = s1 >= _rows(a[PEER_TOPK - 1], n_slabs)
    in2 = s2 >= _rows(b[PEER_TOPK - 1], n_slabs)
    e1 = jnp.where(in1, jnp.exp(s1 - _rows(a[0], n_slabs)), 0.0)
    e2 = jnp.where(in2, jnp.exp(s2 - _rows(b[0], n_slabs)), 0.0) / _rows(z, n_slabs)
    count = lambda m: jnp.sum(jnp.where(m, 1.0, 0.0), axis=0, keepdims=True)
    n_pairs = count(jnp.concatenate(cands, axis=0) >= _rows(tau, 2 * PEER_TOPK))
    tied = jnp.max(jnp.maximum(jnp.maximum(count(in1), count(in2)), n_pairs)) > PEER_TOPK + 0.5
    return e1, e2, tau, tied


def _peer_gates_exact(s1, s2):
    n = s1.shape[1]
    (a, k1), (b, k2) = _extract16([s1, s2])
    cand = jnp.concatenate([a[r:r + 1, :] + b for r in range(PEER_TOPK)], axis=0)
    (best, kc), = _extract16([cand])
    z = jnp.sum(jnp.exp(best - best[0:1, :]), axis=0, keepdims=True)
    e1 = jnp.where(k1 < PEER_TOPK, jnp.exp(s1 - a[0:1, :]), 0.0)
    e2 = jnp.where(k2 < PEER_TOPK, jnp.exp(s2 - b[0:1, :]), 0.0) / z
    taken = jnp.where(kc < PEER_TOPK, 1.0, 0.0)
    r1 = jnp.zeros_like(s1)
    for r in range(PEER_TOPK):
        row_len = jnp.sum(taken[r * PEER_TOPK:(r + 1) * PEER_TOPK], axis=0, keepdims=True)
        r1 = r1 + jnp.where(k1 == r, row_len, 0.0)
    r2 = jnp.where(k2 < PEER_TOPK, -k2, -4.0 * PEER_TOPK)
    return e1, e2, r1, r2, jnp.full((SUBLANES, n), 0.5, F32)


def _peer_kernel(hp_ref, x1_ref, wpqt_ref, sk_ref, u_ref, vt_ref, gf_ref, y_ref,
                 s1_ref, s2_ref, e1_ref, e2_ref, tau_ref, out_ref, *, tm, eb):
    e = pl.program_id(1)
    n_e = pl.num_programs(1)
    hp = hp_ref[...]
    nt = (((1,), (1,)), ((), ()))

    @pl.when(e == 0)
    def _select():
        qt = lax.dot_general(wpqt_ref[...], hp, nt, preferred_element_type=F32).astype(BF16)
        for hh in range(N_PEER_HEADS):
            for c, ref in enumerate((s1_ref, s2_ref)):
                idx = hh * 2 + c
                ref[hh] = jnp.dot(sk_ref[idx], qt[idx * PEER_HALF:(idx + 1) * PEER_HALF, :],
                                  preferred_element_type=F32)

        def head_body(hh, carry):
            for ch in range(tm // LANES):
                sl = slice(ch * LANES, (ch + 1) * LANES)
                s1 = s1_ref[hh, :, sl]
                s2 = s2_ref[hh, :, sl]
                e1, e2, tau, tied = _peer_gates_fast(s1, s2)
                e1_ref[hh, :, sl] = e1
                e2_ref[hh, :, sl] = e2
                tau_ref[hh, :, sl] = tau

                @pl.when(tied)
                def _redo(hh=hh, sl=sl, s1=s1, s2=s2):
                    e1x, e2x, r1, r2, cut = _peer_gates_exact(s1, s2)
                    e1_ref[hh, :, sl] = e1x
                    e2_ref[hh, :, sl] = e2x
                    s1_ref[hh, :, sl] = r1
                    s2_ref[hh, :, sl] = r2
                    tau_ref[hh, :, sl] = cut
            return carry

        lax.fori_loop(0, N_PEER_HEADS, head_body, 0)
        out_ref[...] = jnp.zeros(out_ref.shape, F32)

    act = _gelu(lax.dot_general(u_ref[...], hp, nt, preferred_element_type=F32))
    parts = []
    for ii in range(eb // N_KEYS):
        i = e * (eb // N_KEYS) + ii
        wt = jnp.zeros((N_KEYS, tm), F32)
        for hh in range(N_PEER_HEADS):
            cs = s1_ref[hh, pl.ds(i, 1), :] + s2_ref[hh]
            wt = wt + jnp.where(cs >= tau_ref[hh, 0:1, :],
                                e1_ref[hh, pl.ds(i, 1), :] * e2_ref[hh], 0.0)
        parts.append((wt * act[ii * N_KEYS:(ii + 1) * N_KEYS, :]).astype(BF16))
    gt = jnp.concatenate(parts, axis=0)
    out_ref[...] += jnp.dot(vt_ref[...], gt, preferred_element_type=F32)

    @pl.when(e == n_e - 1)
    def _finish():
        y_ref[...] = _rms(x1_ref[...] + out_ref[...].T, gf_ref[...])


def _peer(hp, x1, wpqt, sk, eu, evt, gf, tm, eb):
    n, d = x1.shape
    n_exp = eu.shape[0]
    row = lambda t, e: (t, 0)
    fixed = lambda t, e: (0, 0)
    head_scr = pltpu.VMEM((N_PEER_HEADS, N_KEYS, tm), F32)
    return pl.pallas_call(
        functools.partial(_peer_kernel, tm=tm, eb=eb),
        grid=(n // tm, n_exp // eb),
        in_specs=[
            pl.BlockSpec((tm, d), row),
            pl.BlockSpec((tm, d), row),
            pl.BlockSpec(wpqt.shape, fixed),
            pl.BlockSpec(sk.shape, lambda t, e: (0, 0, 0)),
            pl.BlockSpec((eb, d), lambda t, e: (e, 0)),
            pl.BlockSpec((d, eb), lambda t, e: (0, e)),
            pl.BlockSpec((1, d), fixed),
        ],
        out_specs=pl.BlockSpec((tm, d), row),
        out_shape=jax.ShapeDtypeStruct((n, d), F32),
        scratch_shapes=[head_scr, head_scr, head_scr, head_scr,
                        pltpu.VMEM((N_PEER_HEADS, 8, tm), F32), pltpu.VMEM((d, tm), F32)],
        compiler_params=_params("parallel", "arbitrary"),
        name="peer",
    )(hp, x1, wpqt, sk, eu, evt, gf)


def _split_w_in(w_in_l):
    a = 3 * ATTN_WIDTH + N_IDX_HEADS * IDX_DIM
    b = a + IDX_DIM + N_IDX_HEADS
    wm = jnp.concatenate([w_in_l[:, :a], w_in_l[:, b:]], axis=1).astype(BF16)
    ws = jnp.pad(w_in_l[:, a:b], ((0, 0), (0, LANES - (b - a)))).astype(BF16)
    return wm, ws


def _prompt_dsa_operands(q, k, v, qi, small, b, t, tq, tk):
    nq = t // tq
    qi_t = qi.reshape(b, nq, tq, N_IDX_HEADS, IDX_DIM).transpose(0, 1, 4, 3, 2)
    qi_t = qi_t.reshape(b, nq, IDX_DIM, N_IDX_HEADS * tq).astype(BF16)
    wi = small[:, IDX_DIM:IDX_DIM + N_IDX_HEADS]
    w_t = wi.reshape(b, nq, tq, N_IDX_HEADS).transpose(0, 1, 3, 2).reshape(b, nq, 1, N_IDX_HEADS * tq)
    ki = small[:, :IDX_DIM].reshape(b, t, IDX_DIM).astype(BF16)
    qt = q.astype(BF16).reshape(b, nq, tq, N_HEADS // 2, 2, HEAD_DIM).transpose(0, 1, 3, 4, 5, 2)
    eye = jnp.eye(2, dtype=BF16)
    qbd = qt[:, :, :, :, :, None, :] * eye[None, None, None, :, None, :, None]
    qbd = qbd.reshape(b, nq, N_HEADS // 2, 2 * HEAD_DIM, 2 * tq)
    kb = k.astype(BF16).reshape(b, t, ATTN_WIDTH)
    vt = v.astype(BF16).reshape(b, t // tk, tk, ATTN_WIDTH).transpose(0, 1, 3, 2)
    return qi_t, w_t, ki, qbd, kb, vt


def _sample_group_attention(qs, ks, vs, qis, smalls, page_table, cache_k_l, cache_v_l, cache_ki_l, bd, tn):
    past = page_table.shape[1] * PAGE_SIZE
    n_rows = N_HEADS * tn
    head_rows = lambda x: jnp.pad(x.reshape(bd, tn, N_HEADS, -1).transpose(0, 2, 1, 3).reshape(bd, n_rows, -1),
                                  ((0, 0), (0, LANES - n_rows), (0, 0)))
    qi_rows = head_rows(qis).astype(BF16)
    w_col = head_rows(smalls[:, IDX_DIM:IDX_DIM + N_IDX_HEADS])
    q_rows = head_rows(qs).astype(BF16)
    ki_new = jnp.pad(smalls[:, :IDX_DIM].reshape(bd, tn, IDX_DIM), ((0, 0), (0, PAGE_SIZE - tn), (0, 0)))
    n_sel = min(MAX_TOPK, (past + tn) // 4)
    keys, keys_new, thr, need = _sample_score(page_table, qi_rows, w_col, ki_new, cache_ki_l, n_sel, tn)
    flat_pad = ((0, 0), (0, (PAGE_SIZE - tn) * N_HEADS), (0, 0))
    k_new = jnp.pad(ks.reshape(bd, n_rows, HEAD_DIM), flat_pad)
    v_new = jnp.pad(vs.reshape(bd, n_rows, HEAD_DIM), flat_pad)
    col = jnp.arange(PAGE_SIZE * N_HEADS)
    spread = (jnp.arange(PAGE_SIZE)[:, None] == col[None, :] // N_HEADS).astype(BF16)
    bias = jnp.where(jnp.arange(LANES)[:, None] // tn == col[None, :] % N_HEADS, 0.0, NEG_BIG).astype(F32)
    earlier = (jnp.arange(PAGE_SIZE)[:, None] < jnp.arange(PAGE_SIZE)[None, :]).astype(BF16)
    tied = (jnp.max(need[:, :, LANES // 2:], axis=(1, 2)) > 0.5).astype(I32)
    out = _sample_attend(page_table, tied, q_rows, keys, keys_new, thr, need, k_new, v_new, spread, bias, earlier,
                         cache_k_l, cache_v_l)
    return out[:, :n_rows].reshape(bd, N_HEADS, tn, HEAD_DIM).transpose(0, 2, 1, 3).reshape(bd * tn, ATTN_WIDTH)


def _sample_gmlp_weights(w_sp_l, bias_p, tn):
    per_chunk = CHUNK // tn
    tril = jnp.tril(jnp.ones((tn, tn), dtype=bool))
    wsp_s = jnp.einsum('ij,gts->gitjs', jnp.eye(per_chunk, dtype=F32),
                       jnp.where(tril[None], w_sp_l[:, :tn, :tn], 0))
    wsp_s = wsp_s.reshape(N_GROUPS, CHUNK, CHUNK).astype(BF16)
    return wsp_s, jnp.tile(bias_p[:tn], (per_chunk, 1))


def kernel(x_prompt, x_sample, cache_k, cache_v, cache_kidx, page_table, norm_mix_g, w_in,
           gv_norm_g, w_sp, b_sp, w_out, norm_ffn_g, w_pq, peer_sub_keys, expert_u, expert_v,
           norm_final_g):
    b, t, d = x_prompt.shape
    bd, tn, _ = x_sample.shape
    depth = w_in.shape[0]
    assert depth == 1 and tn == 8 and t % 256 == 0 and (bd * tn) % CHUNK == 0
    n_pages = page_table.shape[1]
    past = n_pages * PAGE_SIZE
    l = 0

    wm, ws = _split_w_in(w_in[l])
    g_mix = norm_mix_g[l].reshape(1, d)
    gvg = gv_norm_g[l].reshape(1, GMLP_WIDTH)
    lane_group = jnp.arange(GMLP_WIDTH) // HEAD_DIM
    gmean = (lane_group[:, None] == lane_group[None, :]).astype(BF16) * (1.0 / HEAD_DIM)
    invf = ROPE_THETA ** (-jnp.arange(ROPE_HALF, dtype=F32) / ROPE_HALF)
    invf = jnp.tile(invf, LANES // ROPE_HALF).reshape(1, LANES)
    tril = jnp.tril(jnp.ones((CHUNK, CHUNK), dtype=bool))
    wsp_p = jnp.where(tril[None], w_sp[l], 0).astype(BF16)
    bias_p = jnp.repeat(b_sp[l].T, HEAD_DIM, axis=1)
    wsp_s, bias_s = _sample_gmlp_weights(w_sp[l], bias_p, tn)
    wo =w_out[l].astype(BF16)
    g_ffn = norm_ffn_g[l].reshape(1, d)
    wpqt = w_pq[l].T.astype(BF16)
    sk = peer_sub_keys[l].reshape(2 * N_PEER_HEADS, N_KEYS, PEER_HALF).astype(BF16)
    eu = expert_u[l].astype(BF16)
    evt = expert_v[l].T.astype(BF16)
    gf = norm_final_g.reshape(1, d)

    n_p = b * t
    pos_p = jnp.tile(jnp.arange(t, dtype=F32), b).reshape(n_p, 1)
    q, k, v, qi, u, gv, small = _project(x_prompt.reshape(n_p, d), pos_p, invf, g_mix, wm, ws,
                                         gvg, gmean, 256)
    tq, tk = 256, 256
    ops = _prompt_dsa_operands(q, k, v, qi, small, b, t, tq, tk)
    a_t = _dsa_prompt(*ops, tq, tk, min(MAX_TOPK, t // 4))
    a_p = a_t.transpose(0, 2, 1).reshape(n_p, ATTN_WIDTH).astype(BF16)
    x1_p, hp_p = _mix(a_p, u, gv, x_prompt.reshape(n_p, d), wsp_p, bias_p, wo, g_ffn)

    n_s = bd * tn
    pos_s = jnp.tile(past + jnp.arange(tn, dtype=F32), bd).reshape(n_s, 1)
    qs, ks, vs, qis, us, gvs, smalls = _project(x_sample.reshape(n_s, d), pos_s, invf, g_mix, wm,
                                                ws, gvg, gmean, n_s)
    a_s = _sample_group_attention(qs, ks, vs, qis, smalls, page_table, cache_k, cache_v,
                                  cache_kidx, bd, tn).astype(BF16)
    x1_s, hp_s = _mix(a_s, us, gvs, x_sample.reshape(n_s, d), wsp_s, bias_s, wo, g_ffn)

    x1 = jnp.concatenate([x1_p, x1_s], axis=0)
    hp = jnp.concatenate([hp_p, hp_s], axis=0)
    n_all = n_p + n_s
    tm = 640 if n_all % 640 == 0 else 128
    y = _peer(hp, x1, wpqt, sk, eu, evt, gf, tm, 1024)

    heads = (N_HEADS, HEAD_DIM)
    return (y[:n_p].reshape(b, t, d),
            y[n_p:].reshape(bd, tn, d),
            k.reshape(1, b, t, *heads),
            v.reshape(1, b, t, *heads),
            small[:, :IDX_DIM].reshape(1, b, t, IDX_DIM),
            ks.reshape(1, bd, tn, *heads),
            vs.reshape(1, bd, tn, *heads),
            smalls[:, :IDX_DIM].reshape(1, bd, tn, IDX_DIM),
            gvs.reshape(1, bd, tn, N_GROUPS, HEAD_DIM))
```

```python
import functools

import jax
import jax.numpy as jnp
from jax import lax
from jax.experimental import pallas as pl
from jax.experimental.pallas import tpu as pltpu

F32 = jnp.float32
BF16 = jnp.bfloat16
I32 = jnp.int32

EPS = 1e-6
ROPE_THETA = 10000.0
HEAD_DIM = 64
ROPE_HALF = HEAD_DIM // 2
N_HEADS = 8
ATTN_WIDTH = N_HEADS * HEAD_DIM
N_IDX_HEADS = 8
IDX_DIM = 64
MAX_TOPK = 256
N_GROUPS = 8
GMLP_WIDTH = N_GROUPS * HEAD_DIM
CHUNK = 128
PAGE_SIZE = 128
N_PEER_HEADS = 8
N_KEYS = 128
PEER_HALF = 64
PEER_TOPK = 16
LANES = 128

INT_MIN = -(2 ** 31)
NEG_BIG = -1e30
IDX_SCALE = (IDX_DIM ** -0.5) * (N_IDX_HEADS ** -0.5)
ATTN_SCALE = HEAD_DIM ** -0.5
LOG2E = 1.4426950408889634

VMEM_LIMIT_BYTES = 56 * 1024 * 1024


def _params(*sem):
    return pltpu.CompilerParams(dimension_semantics=sem, vmem_limit_bytes=VMEM_LIMIT_BYTES)


def _gelu(x):
    cdf = 0.5 * (1.0 + jnp.tanh(0.7978845608028654 * (x + 0.044715 * (x * x * x))))
    return x * cdf


def _rms(x, g):
    return x * lax.rsqrt(jnp.mean(x * x, axis=-1, keepdims=True) + EPS) * g


def _split3(x):
    a = x.astype(BF16)
    r = x - a.astype(F32)
    b = r.astype(BF16)
    c = (r - b.astype(F32)).astype(BF16)
    return a, b, c


def _dot_f32_by_bf16(x, m):
    a, b, c = _split3(x)
    d = lambda p: jnp.dot(p, m, preferred_element_type=F32)
    return d(a) + d(b) + d(c)


def _code_to_f32(code):
    bits = jnp.where(code < 0, code ^ 0x7FFFFFFF, code)
    return lax.bitcast_convert_type(bits, F32)


LOWEST_FINITE_CODE = -(2 ** 31) + 0x00800000


COUNT_ROWS = 64


def _count_partial(mask):
    rows, n = mask.shape
    ones = jnp.where(mask, 1.0, 0.0)
    return jnp.sum(ones.reshape(rows // COUNT_ROWS, COUNT_ROWS, n), axis=0)


def _kth_largest_key(count_ge, k, shape):
    c0 = count_ge(jnp.zeros(shape, I32))
    t0 = jnp.where(c0 >= k, 0, INT_MIN).astype(I32)

    def body(i, t):
        cand = t + lax.shift_left(jnp.int32(1), 30 - i)
        return jnp.where(count_ge(cand) >= k, cand, t)

    return lax.fori_loop(0, 31, body, t0)


def _project_kernel(x_ref, pos_ref, invf_ref, g_ref, wm_ref, ws_ref, gvg_ref, gmean_ref,
                    q_ref, k_ref, v_ref, qi_ref, u_ref, gv_ref, small_ref):
    h = _rms(x_ref[...], g_ref[...]).astype(BF16)
    ang = pos_ref[...] * invf_ref[...]
    cos = jnp.cos(ang)
    sin = jnp.sin(ang)
    lane = lax.broadcasted_iota(I32, ang.shape, 1)
    first = (lane % HEAD_DIM) < ROPE_HALF
    sin_signed = jnp.where(first, -sin, sin)

    def rope(xc):
        partner = jnp.where(first, pltpu.roll(xc, LANES - ROPE_HALF, 1), pltpu.roll(xc, ROPE_HALF, 1))
        return xc * cos + partner * sin_signed

    def proj(c0):
        return jnp.dot(h, wm_ref[:, c0:c0 + ATTN_WIDTH], preferred_element_type=F32)

    for ref, c0 in ((q_ref, 0), (k_ref, ATTN_WIDTH), (qi_ref, 3 * ATTN_WIDTH)):
        p = proj(c0)
        for c in range(ATTN_WIDTH // LANES):
            ref[:, c * LANES:(c + 1) * LANES] = rope(p[:, c * LANES:(c + 1) * LANES])
    v_ref[...] = proj(2 * ATTN_WIDTH)
    u_ref[...] = _gelu(proj(4 * ATTN_WIDTH))
    gl = _gelu(proj(5 * ATTN_WIDTH))
    ms = _dot_f32_by_bf16(gl * gl, gmean_ref[...])
    gv_ref[...] = gl * lax.rsqrt(ms + EPS) * gvg_ref[...]
    sm = jnp.dot(h, ws_ref[...], preferred_element_type=F32)
    small_ref[...] = jnp.where(lane < IDX_DIM, rope(sm), sm)


def _project(x2d, pos, invf, g, wm, ws, gvg, gmean, tm):
    n, d = x2d.shape
    row = lambda i: (i, 0)
    fixed = lambda i: (0, 0)
    wide = jax.ShapeDtypeStruct((n, ATTN_WIDTH), F32)
    return pl.pallas_call(
        _project_kernel,
        grid=(n // tm,),
        in_specs=[
            pl.BlockSpec((tm, d), row),
            pl.BlockSpec((tm, 1), row),
            pl.BlockSpec((1, LANES), fixed),
            pl.BlockSpec((1, d), fixed),
            pl.BlockSpec(wm.shape, fixed),
            pl.BlockSpec(ws.shape, fixed),
            pl.BlockSpec((1, GMLP_WIDTH), fixed),
            pl.BlockSpec(gmean.shape, fixed),
        ],
        out_specs=[pl.BlockSpec((tm, ATTN_WIDTH), row)] * 6 + [pl.BlockSpec((tm, LANES), row)],
        out_shape=[wide] * 6 + [jax.ShapeDtypeStruct((n, LANES), F32)],
        compiler_params=_params("parallel"),
        name="project",
    )(x2d, pos, invf, g, wm, ws, gvg, gmean)


def _dsa_prompt_kernel(qi_ref, w_ref, ki_ref, qbd_ref, k_ref, vt_ref, out_ref, keys_ref, acc_ref, lg_ref,
                       *, tq, tk, n_sel):
    j = pl.program_id(1)
    n_tiles = pl.num_programs(1)
    sub = tq // tk
    nkb = (j + 1) * sub
    q0 = j * tq
    qi = qi_ref[0, 0]
    w = w_ref[0, 0] * IDX_SCALE
    qpos = q0 + lax.broadcasted_iota(I32, (tk, tq), 1)
    krow = lax.broadcasted_iota(I32, (tk, tq), 0)

    def score_body(kq, carry):
        for u in range(sub):
            r0 = pl.multiple_of(kq * tq, tq) + u * tk
            d = jnp.dot(ki_ref[0, pl.ds(r0, tk), :], qi, preferred_element_type=F32)
            s = jnp.zeros((tk, tq), F32)
            for h in range(N_IDX_HEADS):
                s = s + jnp.maximum(d[:, h * tq:(h + 1) * tq], 0.0) * w[:, h * tq:(h + 1) * tq]
            keys_ref[pl.ds(r0, tk), :] = jnp.where(krow + r0 <= qpos, s, -jnp.inf)
        return carry

    lax.fori_loop(0, j + 1, score_body, 0)

    @pl.when(jnp.logical_and((j + 1) % 2 == 1, j + 1 < n_tiles))
    def _blank():
        keys_ref[pl.ds(pl.multiple_of((j + 1) * tq, tq), tq), :] = jnp.full((tq, tq), -jnp.inf, F32)

    def count_ge(code):
        cand = _code_to_f32(code)

        def body(kb, acc):
            blk = keys_ref[pl.ds(pl.multiple_of(kb * 2 * tq, 2 * tq), 2 * tq), :]
            return acc + _count_partial(blk >= cand)
        acc = lax.fori_loop(0, (j + 2) // 2, body, jnp.zeros((COUNT_ROWS, tq), F32))
        return jnp.sum(acc, axis=0, keepdims=True)

    cut = jnp.maximum(_kth_largest_key(count_ge, n_sel, (1, tq)), LOWEST_FINITE_CODE)
    thr = _code_to_f32(cut)

    @pl.when(jnp.max(count_ge(cut)) > n_sel + 0.5)
    def _break_ties():
        blk_rows = 2 * tq
        n_blocks = (j + 2) // 2
        thr_next = _code_to_f32(cut + 1)
        need = n_sel - count_ge(cut + 1)

        def load(kb):
            rows = pl.ds(pl.multiple_of(kb * blk_rows, blk_rows), blk_rows)
            blk = keys_ref[rows, :]
            above = jnp.where(blk >= thr, jnp.where(blk < thr_next, blk - thr, -jnp.inf), -jnp.inf)
            return rows, blk, above

        def count_above_ge(code):
            cand = _code_to_f32(code)

            def body(kb, acc):
                return acc + _count_partial(load(kb)[2] >= cand)
            acc = lax.fori_loop(0, n_blocks, body, jnp.zeros((COUNT_ROWS, tq), F32))
            return jnp.sum(acc, axis=0, keepdims=True)

        cut2 = _kth_largest_key(count_above_ge, need, (1, tq))
        thr2 = _code_to_f32(cut2)
        need2 = need - count_above_ge(cut2 + 1)
        earlier = (lax.broadcasted_iota(I32, (blk_rows, blk_rows), 1)
                   < lax.broadcasted_iota(I32, (blk_rows, blk_rows), 0))
        earlier = jnp.where(earlier, 1.0, 0.0).astype(BF16)

        def body(kb, seen):
            rows, blk, above = load(kb)
            eq = jnp.where(above == thr2, 1.0, 0.0)
            before = jnp.dot(earlier, eq.astype(BF16), preferred_element_type=F32) + seen
            late = eq * before >= jnp.maximum(need2, 0.5)
            keys_ref[rows, :] = jnp.where(above < thr2, jnp.where(above >= 0.0, -jnp.inf, blk),
                                          jnp.where(late, -jnp.inf, blk))
            return seen + jnp.sum(eq, axis=0, keepdims=True)

        lax.fori_loop(0, n_blocks, body, jnp.zeros((1, tq), F32))

    n_pairs = N_HEADS // 2
    acc_ref[...] = jnp.zeros(acc_ref.shape, F32)

    def logits(kb):
        r0 = pl.multiple_of(kb * tk, tk)
        return [jnp.dot(k_ref[0, pl.ds(r0, tk), p * LANES:(p + 1) * LANES], qbd_ref[0, 0, p],
                        preferred_element_type=F32) for p in range(n_pairs)]

    def stash(slot, lgs):
        for p in range(n_pairs):
            lg_ref[slot, :, p * 2 * tq:(p + 1) * 2 * tq] = lgs[p]

    stash(0, logits(0))

    def att_body(kb, carry):
        ms, ls = carry
        r0 = pl.multiple_of(kb * tk, tk)
        slot = kb % 2
        nxt = logits(jnp.minimum(kb + 1, nkb - 1))
        sel = keys_ref[pl.ds(r0, tk), :] >= thr
        new_ms, new_ls = [], []
        for p in range(n_pairs):
            rows = slice(p * LANES, (p + 1) * LANES)
            lg = lg_ref[slot, :, p * 2 * tq:(p + 1) * 2 * tq] * (ATTN_SCALE * LOG2E)
            lm = jnp.concatenate([jnp.where(sel, lg[:, :tq], NEG_BIG),
                                  jnp.where(sel, lg[:, tq:], NEG_BIG)], axis=1)
            m_new = jnp.maximum(ms[p], jnp.max(lm, axis=0, keepdims=True))
            pe = jnp.exp2(lm - m_new)
            alpha = jnp.exp2(ms[p] - m_new)
            new_ls.append(alpha * ls[p] + jnp.sum(pe, axis=0, keepdims=True))
            new_ms.append(m_new)
            pb = pe.astype(BF16)
            vtb = vt_ref[0, kb, rows, :]
            pv0 = jnp.dot(vtb[:HEAD_DIM], pb[:, :tq], preferred_element_type=F32)
            pv1 = jnp.dot(vtb[HEAD_DIM:], pb[:, tq:], preferred_element_type=F32)
            acc = acc_ref[rows, :]
            acc_ref[rows, :] = jnp.concatenate([alpha[:, :tq] * acc[:HEAD_DIM] + pv0,
                                                alpha[:, tq:] * acc[HEAD_DIM:] + pv1], axis=0)
        stash(1 - slot, nxt)
        return tuple(new_ms), tuple(new_ls)

    init = (tuple(jnp.full((1, 2 * tq), NEG_BIG, F32) for _ in range(n_pairs)),
            tuple(jnp.zeros((1, 2 * tq), F32) for _ in range(n_pairs)))
    _, ls = lax.fori_loop(0, nkb, att_body, init)
    for p in range(n_pairs):
        rows = slice(p * LANES, (p + 1) * LANES)
        acc = acc_ref[rows, :]
        out_ref[0, rows, :] = jnp.concatenate(
            [acc[:HEAD_DIM] / ls[p][:, :tq], acc[HEAD_DIM:] / ls[p][:, tq:]], axis=0)


def _dsa_prompt(qi_t, w_t, ki, qbd, k, vt, tq, tk, n_sel):
    b, nq = qi_t.shape[:2]
    t = k.shape[1]
    tile = lambda bb, j: (bb, j, 0, 0)
    whole3 = lambda bb, j: (bb, 0, 0)
    return pl.pallas_call(
        functools.partial(_dsa_prompt_kernel, tq=tq, tk=tk, n_sel=n_sel),
        grid=(b, nq),
        in_specs=[
            pl.BlockSpec((1, 1, IDX_DIM, N_IDX_HEADS * tq), tile),
            pl.BlockSpec((1, 1, 1, N_IDX_HEADS * tq), tile),
            pl.BlockSpec((1, t, IDX_DIM), whole3, pipeline_mode=pl.Buffered(1)),
            pl.BlockSpec((1, 1, N_HEADS // 2, LANES, 2 * tq), lambda bb, j: (bb, j, 0, 0, 0)),
            pl.BlockSpec((1, t, ATTN_WIDTH), whole3, pipeline_mode=pl.Buffered(1)),
            pl.BlockSpec((1, t // tk, ATTN_WIDTH, tk), lambda bb, j: (bb, 0, 0, 0),
                         pipeline_mode=pl.Buffered(1)),
        ],
        out_specs=pl.BlockSpec((1, ATTN_WIDTH, tq), lambda bb, j: (bb, 0, j)),
        out_shape=jax.ShapeDtypeStruct((b, ATTN_WIDTH, t), F32),
        scratch_shapes=[pltpu.VMEM((t, tq), F32), pltpu.VMEM((ATTN_WIDTH, tq), F32),
                        pltpu.VMEM((2, tk, N_HEADS * tq), F32)],
        compiler_params=_params("parallel", "parallel"),
        name="dsa_prompt",
    )(qi_t, w_t, ki, qbd, k, vt)


PAGES_PER_STEP = 8
STEP_KEYS = PAGES_PER_STEP * PAGE_SIZE


def _sample_score_kernel(pt_ref, qi_ref, w_ref, kinew_ref, *rest, n_sel, n_new):
    pages = rest[:PAGES_PER_STEP]
    keys_ref, keysnew_ref, thr_ref, need_ref, scr = rest[PAGES_PER_STEP:]
    s = pl.program_id(1)
    n_steps = pl.num_programs(1)
    qi = qi_ref[...]
    w = w_ref[...] * IDX_SCALE

    def score(ki_t):
        d = jnp.dot(qi, ki_t.astype(BF16), preferred_element_type=F32)
        sc = jnp.maximum(d, 0.0) * w
        tot = jnp.zeros((n_new, ki_t.shape[1]), F32)
        for h in range(N_IDX_HEADS):
            tot = tot + sc[h * n_new:(h + 1) * n_new]
        return tot

    key = score(jnp.concatenate([page[...] for page in pages], axis=1))
    keys_ref[...] = key
    scr[s] = key

    @pl.when(s == n_steps - 1)
    def _finish():
        key = score(kinew_ref[...])
        qrow = lax.broadcasted_iota(I32, key.shape, 0)
        kcol = lax.broadcasted_iota(I32, key.shape, 1)
        key = jnp.where(kcol <= qrow, key, -jnp.inf)
        keysnew_ref[...] = key
        scr[n_steps] = jnp.concatenate(
            [key, jnp.full((n_new, STEP_KEYS - PAGE_SIZE), -jnp.inf, F32)], axis=1)

        def count_ge(code):
            cand = _code_to_f32(code)
            acc = jnp.zeros((n_new, STEP_KEYS), F32)
            for slab in range(scr.shape[0]):
                acc = acc + jnp.where(scr[slab] >= cand, 1.0, 0.0)
            return jnp.sum(acc, axis=1, keepdims=True)

        cut = jnp.maximum(_kth_largest_key(count_ge, n_sel, (n_new, 1)), LOWEST_FINITE_CODE)
        thr = _code_to_f32(cut)
        thr_next = _code_to_f32(cut + 1)
        need = n_sel - count_ge(cut + 1)

        def count_above_ge(code):
            cand = _code_to_f32(code)
            acc = jnp.zeros((n_new, STEP_KEYS), F32)
            for slab in range(scr.shape[0]):
                sc = scr[slab]
                above = jnp.where(sc >= thr, jnp.where(sc < thr_next, sc - thr, -jnp.inf), -jnp.inf)
                acc = acc + jnp.where(above >= cand, 1.0, 0.0)
            return jnp.sum(acc, axis=1, keepdims=True)

        cut2 = _kth_largest_key(count_above_ge, need, (n_new, 1))
        need2 = need - count_above_ge(cut2 + 1)
        surplus = count_ge(cut) - n_sel
        lane = lax.broadcasted_iota(I32, thr_ref.shape, 1)
        wide = lambda x: jnp.broadcast_to(x, thr_ref.shape)
        thr_ref[...] = jnp.where(lane < LANES // 2, wide(thr), wide(thr_next))
        need_ref[...] = jnp.where(lane < LANES // 4, wide(_code_to_f32(cut2)),
                                  jnp.where(lane < LANES // 2, wide(need2), wide(surplus)))


def _page_spec(tail, k):
    zeros = (0,) * len(tail)
    return pl.BlockSpec((None, None) + tail, lambda b, s, pt, *_: (0, pt[b, s * PAGES_PER_STEP + k]) + zeros)


def _sample_score(page_table, qi_rows, w_col, ki_new, cache_ki, n_sel, n_new):
    nb, n_pages = page_table.shape
    n_steps = n_pages // PAGES_PER_STEP
    per_b = lambda b, s, pt: (b, 0, 0)
    grid_spec = pltpu.PrefetchScalarGridSpec(
        num_scalar_prefetch=1,
        grid=(nb, n_steps),
        in_specs=[
            pl.BlockSpec((None, LANES, IDX_DIM), per_b),
            pl.BlockSpec((None, LANES, 1), per_b),
            pl.BlockSpec((None, IDX_DIM, PAGE_SIZE), per_b),
        ] + [_page_spec((IDX_DIM, PAGE_SIZE), k) for k in range(PAGES_PER_STEP)],
        out_specs=[
            pl.BlockSpec((None, n_new, STEP_KEYS), lambda b, s, pt: (b, 0, s)),
            pl.BlockSpec((None, n_new, PAGE_SIZE), per_b),
            pl.BlockSpec((None, n_new, LANES), per_b),
            pl.BlockSpec((None, n_new, LANES), per_b),
        ],
        scratch_shapes=[pltpu.VMEM((n_steps + 1, n_new, STEP_KEYS), F32)],
    )
    return pl.pallas_call(
        functools.partial(_sample_score_kernel, n_sel=n_sel, n_new=n_new),
        grid_spec=grid_spec,
        out_shape=[
            jax.ShapeDtypeStruct((nb, n_new, n_pages * PAGE_SIZE), F32),
            jax.ShapeDtypeStruct((nb, n_new, PAGE_SIZE), F32),
            jax.ShapeDtypeStruct((nb, n_new, LANES), F32),
            jax.ShapeDtypeStruct((nb, n_new, LANES), F32),
        ],
        compiler_params=_params("parallel", "arbitrary"),
        name="sample_score",
    )(page_table, qi_rows, w_col, ki_new, *([cache_ki] * PAGES_PER_STEP))


def _sample_attend_kernel(pt_ref, tied_ref, q_ref, keys_ref, keysnew_ref, thr_ref, need_ref, knew_ref, vnew_ref,
                          diag_ref, fold_ref, earlier_ref, *rest):
    kpages = rest[:PAGES_PER_STEP]
    vpages = rest[PAGES_PER_STEP:2 * PAGES_PER_STEP]
    out_ref, m_ref, l_ref, acc_ref, seen_ref = rest[2 * PAGES_PER_STEP:]
    s = pl.program_id(1)
    n_steps = pl.num_programs(1)
    n_new = keys_ref.shape[0]
    nt = (((1,), (1,)), ((), ()))

    @pl.when(s == 0)
    def _init():
        m_ref[...] = jnp.full(m_ref.shape, NEG_BIG, F32)
        l_ref[...] = jnp.zeros(l_ref.shape, F32)
        acc_ref[...] = jnp.zeros(acc_ref.shape, F32)
        seen_ref[...] = jnp.zeros(seen_ref.shape, F32)

    q = q_ref[...]
    thr = thr_ref[:, 0:1]
    thr_next = thr_ref[:, LANES // 2:LANES // 2 + 1]
    thr2 = need_ref[:, 0:1]
    need2 = need_ref[:, LANES // 4:LANES // 4 + 1]

    tied = tied_ref[pl.program_id(0)] != 0

    def selected_in_order(keyblk):
        above = jnp.where(keyblk >= thr, jnp.where(keyblk < thr_next, keyblk - thr, -jnp.inf), -jnp.inf)
        eq = jnp.where(above == thr2, 1.0, 0.0)
        seen = seen_ref[...]
        before = jnp.dot(eq.astype(BF16), earlier_ref[...], preferred_element_type=F32) + seen
        seen_ref[...] = seen + jnp.sum(eq, axis=1, keepdims=True)
        return jnp.where(keyblk >= thr_next, 1.0,
                         jnp.where(above > thr2, 1.0, jnp.where(before < need2, eq, 0.0)))

    def selected(keys):
        def in_order(kk):
            return jnp.concatenate([selected_in_order(kk[:, i:i + PAGE_SIZE])
                                    for i in range(0, kk.shape[1], PAGE_SIZE)], axis=1)
        return lax.cond(tied, in_order, lambda kk: jnp.where(kk >= thr, 1.0, 0.0), keys)

    def accumulate(k_t, v_t, chosen):
        lg = jnp.dot(q, k_t.astype(BF16), preferred_element_type=F32) * (ATTN_SCALE * LOG2E)
        sel = jnp.concatenate([chosen] * (LANES // n_new), axis=0) > 0.5
        lm = jnp.where(sel, lg, NEG_BIG)
        m = m_ref[...]
        m_new = jnp.maximum(m, jnp.max(lm, axis=1, keepdims=True))
        pe = jnp.exp2(lm - m_new)
        alpha = jnp.exp2(m - m_new)
        l_ref[...] = alpha * l_ref[...] + jnp.sum(pe, axis=1, keepdims=True)
        pv = lax.dot_general(pe.astype(BF16), v_t.astype(BF16), nt, preferred_element_type=F32)
        acc_ref[...] = alpha * acc_ref[...] + pv
        m_ref[...] = m_new

    flat = (ATTN_WIDTH, PAGE_SIZE)
    accumulate(jnp.concatenate([p[...].reshape(flat) for p in kpages], axis=1),
               jnp.concatenate([p[...].reshape(flat) for p in vpages], axis=1),
               selected(keys_ref[...]))

    @pl.when(s == n_steps - 1)
    def _finish():
        accumulate(knew_ref[...], vnew_ref[...], selected(keysnew_ref[...]))
        own_head = acc_ref[...] / l_ref[...] * diag_ref[...]
        out_ref[...] = _dot_f32_by_bf16(own_head, fold_ref[...])


def _sample_attend(page_table, tied, q_rows, keys, keys_new, thr, need, k_new, v_new, diag, fold, earlier,
                   cache_k_t, cache_v_t):
    nb, n_pages = page_table.shape
    n_steps = n_pages // PAGES_PER_STEP
    n_new = keys.shape[1]
    per_b = lambda b, s, pt, tied: (b, 0, 0)
    fixed = lambda b, s, pt, tied: (0, 0)
    page = (N_HEADS, HEAD_DIM, PAGE_SIZE)
    grid_spec = pltpu.PrefetchScalarGridSpec(
        num_scalar_prefetch=2,
        grid=(nb, n_steps),
        in_specs=[
            pl.BlockSpec((None, LANES, ATTN_WIDTH), per_b),
            pl.BlockSpec((None, n_new, STEP_KEYS), lambda b, s, pt, tied: (b, 0, s)),
            pl.BlockSpec((None, n_new, PAGE_SIZE), per_b),
            pl.BlockSpec((None, n_new, LANES), per_b),
            pl.BlockSpec((None, n_new, LANES), per_b),
            pl.BlockSpec((None, ATTN_WIDTH, PAGE_SIZE), per_b),
            pl.BlockSpec((None, ATTN_WIDTH, PAGE_SIZE), per_b),
            pl.BlockSpec(diag.shape, fixed),
            pl.BlockSpec(fold.shape, fixed),
            pl.BlockSpec(earlier.shape, fixed),
        ] + [_page_spec(page, k) for k in range(PAGES_PER_STEP)] * 2,
        out_specs=pl.BlockSpec((None, LANES, HEAD_DIM), per_b),
        scratch_shapes=[pltpu.VMEM((LANES, 1), F32), pltpu.VMEM((LANES, 1), F32),
                        pltpu.VMEM((LANES, ATTN_WIDTH), F32), pltpu.VMEM((n_new, 1), F32)],
    )
    return pl.pallas_call(
        _sample_attend_kernel,
        grid_spec=grid_spec,
        out_shape=jax.ShapeDtypeStruct((nb, LANES, HEAD_DIM), F32),
        compiler_params=_params("parallel", "arbitrary"),
        name="sample_attend",
    )(page_table, tied, q_rows, keys, keys_new, thr, need, k_new, v_new, diag, fold, earlier,
      *([cache_k_t] * PAGES_PER_STEP), *([cache_v_t] * PAGES_PER_STEP))


def _mix_kernel(a_ref, u_ref, gv_ref, x_ref, wsp_ref, bias_ref, wo_ref, g_ref, x1_ref, hp_ref):
    gvb = gv_ref[...].astype(BF16)
    group = lax.broadcasted_iota(I32, gvb.shape, 1) // HEAD_DIM
    mixed = bias_ref[...]
    for g in range(N_GROUPS):
        r = jnp.dot(wsp_ref[g], gvb, preferred_element_type=F32)
        mixed = mixed + jnp.where(group == g, r, 0.0)
    gm = (u_ref[...] * mixed).astype(BF16)
    mix = (jnp.dot(a_ref[...], wo_ref[:ATTN_WIDTH, :], preferred_element_type=F32)
           + jnp.dot(gm, wo_ref[ATTN_WIDTH:, :], preferred_element_type=F32))
    x1 = x_ref[...] + mix
    x1_ref[...] = x1
    hp_ref[...] = _rms(x1, g_ref[...]).astype(BF16)


def _mix(a, u, gv, x, wsp, bias, wo, g):
    n, d = x.shape
    row = lambda i: (i, 0)
    fixed = lambda i: (0, 0)
    return pl.pallas_call(
        _mix_kernel,
        grid=(n // CHUNK,),
        in_specs=[
            pl.BlockSpec((CHUNK, ATTN_WIDTH), row),
            pl.BlockSpec((CHUNK, GMLP_WIDTH), row),
            pl.BlockSpec((CHUNK, GMLP_WIDTH), row),
            pl.BlockSpec((CHUNK, d), row),
            pl.BlockSpec(wsp.shape, lambda i: (0, 0, 0)),
            pl.BlockSpec(bias.shape, fixed),
            pl.BlockSpec(wo.shape, fixed),
            pl.BlockSpec((1, d), fixed),
        ],
        out_specs=[pl.BlockSpec((CHUNK, d), row), pl.BlockSpec((CHUNK, d), row)],
        out_shape=[jax.ShapeDtypeStruct((n, d), F32), jax.ShapeDtypeStruct((n, d), BF16)],
        compiler_params=_params("parallel"),
        name="mix",
    )(a, u, gv, x, wsp, bias, wo, g)


def _col_reduce(x, op):
    slabs = [x[i:i + 8] for i in range(0, x.shape[0], 8)]
    while len(slabs) > 1:
        nxt = [op(slabs[i], slabs[i + 1]) for i in range(0, len(slabs) - 1, 2)]
        if len(slabs) % 2:
            nxt.append(slabs[-1])
        slabs = nxt
    red = jnp.max if op is jnp.maximum else jnp.min
    return red(slabs[0], axis=0, keepdims=True)


def _extract16(arrays):
    rows, n = arrays[0].shape
    ridx = lax.broadcasted_iota(I32, (rows, n), 0).astype(F32)
    r16 = lax.broadcasted_iota(I32, (PEER_TOPK, n), 0)

    def body(r, carry):
        out = []
        for cur, vals, rank in carry:
            mx = _col_reduce(cur, jnp.maximum)
            hit = ridx == _col_reduce(jnp.where(cur == mx, ridx, float(rows)), jnp.minimum)
            out.append((jnp.where(hit, -jnp.inf, cur), jnp.where(r16 == r, mx, vals),
                        jnp.where(hit, lax.convert_element_type(r, F32), rank)))
        return tuple(out)

    init = tuple((s, jnp.zeros((PEER_TOPK, n), F32), jnp.full((rows, n), float(PEER_TOPK), F32))
                 for s in arrays)
    res = lax.fori_loop(0, PEER_TOPK, body, init)
    return [(vals, rank) for _, vals, rank in res]


def _batcher_pairs(n):
    pairs = []

    def merge(lo, hi, r):
        step = r * 2
        if step < hi - lo:
            merge(lo, hi, step)
            merge(lo + r, hi, step)
            pairs.extend((i, i + r) for i in range(lo + r, hi - r, step))
        else:
            pairs.append((lo, lo + r))

    def sort(lo, hi):
        if hi - lo >= 1:
            mid = lo + (hi - lo) // 2
            sort(lo, mid)
            sort(mid + 1, hi)
            merge(lo, hi, 1)

    sort(0, n - 1)
    return pairs


_SORT16 = _batcher_pairs(PEER_TOPK)
SUBLANES = 8


def _compare_exchange(v, i, j):
    v[i], v[j] = jnp.maximum(v[i], v[j]), jnp.minimum(v[i], v[j])


def _merge_top16(a, b):
    v = [jnp.maximum(a[i], b[PEER_TOPK - 1 - i]) for i in range(PEER_TOPK)]
    d = PEER_TOPK // 2
    while d:
        for i in range(PEER_TOPK):
            if not i & d:
                _compare_exchange(v, i, i + d)
        d //= 2
    return v


def _fold_sublanes(v):
    for sh in (4, 2, 1):
        v = _merge_top16(v, [pltpu.roll(x, sh, 0) for x in v])
    return v


def _top16_values(x):
    v = [x[i * SUBLANES:(i + 1) * SUBLANES] for i in range(x.shape[0] // SUBLANES)]
    for i, j in _SORT16:
        _compare_exchange(v, i, j)
    return _fold_sublanes(v)


def _rows(slab, n_slabs):
    return jnp.concatenate([slab] * n_slabs, axis=0)


def _peer_gates_fast(s1, s2):
    n_slabs = s1.shape[0] // SUBLANES
    a = _top16_values(s1)
    b = _top16_values(s2)
    sub = lax.broadcasted_iota(I32, a[0].shape, 0)

    def spread(vals, g):
        out = vals[g * SUBLANES + SUBLANES - 1]
        for t in range(SUBLANES - 2, -1, -1):
            out = jnp.where(sub == t, vals[g * SUBLANES + t], out)
        return out

    b_lo, b_hi = spread(b, 0), spread(b, 1)
    cands = [x + b_lo for x in a] + [x + b_hi for x in a]
    best = _fold_sublanes(_merge_top16(cands[:PEER_TOPK], cands[PEER_TOPK:]))
    tau = best[PEER_TOPK - 1]
    z = functools.reduce(jnp.add, [jnp.exp(x - best[0]) for x in best])
    in1 = s1 >= _rows(a[PEER_TOPK - 1], n_slabs)
    in2 = s2 >= _rows(b[PEER_TOPK - 1], n_slabs)
    e1 = jnp.where(in1, jnp.exp(s1 - _rows(a[0], n_slabs)), 0.0)
    e2 = jnp.where(in2, jnp.exp(s2 - _rows(b[0], n_slabs)), 0.0) / _rows(z, n_slabs)
    count = lambda m: jnp.sum(jnp.where(m, 1.0, 0.0), axis=0, keepdims=True)
    n_pairs = count(jnp.concatenate(cands, axis=0) >= _rows(tau, 2 * PEER_TOPK))
    tied = jnp.max(jnp.maximum(jnp.maximum(count(in1), count(in2)), n_pairs)) > PEER_TOPK + 0.5
    return e1, e2, tau, tied


def _peer_gates_exact(s1, s2):
    n = s1.shape[1]
    (a, k1), (b, k2) = _extract16([s1, s2])
    cand = jnp.concatenate([a[r:r + 1, :] + b for r in range(PEER_TOPK)], axis=0)
    (best, kc), = _extract16([cand])
    z = jnp.sum(jnp.exp(best - best[0:1, :]), axis=0, keepdims=True)
    e1 = jnp.where(k1 < PEER_TOPK, jnp.exp(s1 - a[0:1, :]), 0.0)
    e2 = jnp.where(k2 < PEER_TOPK, jnp.exp(s2 - b[0:1, :]), 0.0) / z
    taken = jnp.where(kc < PEER_TOPK, 1.0, 0.0)
    r1 = jnp.zeros_like(s1)
    for r in range(PEER_TOPK):
        row_len = jnp.sum(taken[r * PEER_TOPK:(r + 1) * PEER_TOPK], axis=0, keepdims=True)
        r1 = r1 + jnp.where(k1 == r, row_len, 0.0)
    r2 = jnp.where(k2 < PEER_TOPK, -k2, -4.0 * PEER_TOPK)
    return e1, e2, r1, r2, jnp.full((SUBLANES, n), 0.5, F32)


def _peer_kernel(hp_ref, x1_ref, wpqt_ref, sk_ref, u_ref, vt_ref, gf_ref, y_ref,
                 s1_ref, s2_ref, e1_ref, e2_ref, tau_ref, out_ref, *, tm, eb):
    e = pl.program_id(1)
    n_e = pl.num_programs(1)
    hp = hp_ref[...]
    nt = (((1,), (1,)), ((), ()))

    @pl.when(e == 0)
    def _select():
        qt = lax.dot_general(wpqt_ref[...], hp, nt, preferred_element_type=F32).astype(BF16)
        for hh in range(N_PEER_HEADS):
            for c, ref in enumerate((s1_ref, s2_ref)):
                idx = hh * 2 + c
                ref[hh] = jnp.dot(sk_ref[idx], qt[idx * PEER_HALF:(idx + 1) * PEER_HALF, :],
                                  preferred_element_type=F32)

        def head_body(hh, carry):
            for ch in range(tm // LANES):
                sl = slice(ch * LANES, (ch + 1) * LANES)
                s1 = s1_ref[hh, :, sl]
                s2 = s2_ref[hh, :, sl]
                e1, e2, tau, tied = _peer_gates_fast(s1, s2)
                e1_ref[hh, :, sl] = e1
                e2_ref[hh, :, sl] = e2
                tau_ref[hh, :, sl] = tau

                @pl.when(tied)
                def _redo(hh=hh, sl=sl, s1=s1, s2=s2):
                    e1x, e2x, r1, r2, cut = _peer_gates_exact(s1, s2)
                    e1_ref[hh, :, sl] = e1x
                    e2_ref[hh, :, sl] = e2x
                    s1_ref[hh, :, sl] = r1
                    s2_ref[hh, :, sl] = r2
                    tau_ref[hh, :, sl] = cut
            return carry

        lax.fori_loop(0, N_PEER_HEADS, head_body, 0)
        out_ref[...] = jnp.zeros(out_ref.shape, F32)

    act = _gelu(lax.dot_general(u_ref[...], hp, nt, preferred_element_type=F32))
    parts = []
    for ii in range(eb // N_KEYS):
        i = e * (eb // N_KEYS) + ii
        wt = jnp.zeros((N_KEYS, tm), F32)
        for hh in range(N_PEER_HEADS):
            cs = s1_ref[hh, pl.ds(i, 1), :] + s2_ref[hh]
            wt = wt + jnp.where(cs >= tau_ref[hh, 0:1, :],
                                e1_ref[hh, pl.ds(i, 1), :] * e2_ref[hh], 0.0)
        parts.append((wt * act[ii * N_KEYS:(ii + 1) * N_KEYS, :]).astype(BF16))
    gt = jnp.concatenate(parts, axis=0)
    out_ref[...] += jnp.dot(vt_ref[...], gt, preferred_element_type=F32)

    @pl.when(e == n_e - 1)
    def _finish():
        y_ref[...] = _rms(x1_ref[...] + out_ref[...].T, gf_ref[...])


def _peer(hp, x1, wpqt, sk, eu, evt, gf, tm, eb):
    n, d = x1.shape
    n_exp = eu.shape[0]
    row = lambda t, e: (t, 0)
    fixed = lambda t, e: (0, 0)
    head_scr = pltpu.VMEM((N_PEER_HEADS, N_KEYS, tm), F32)
    return pl.pallas_call(
        functools.partial(_peer_kernel, tm=tm, eb=eb),
        grid=(n // tm, n_exp // eb),
        in_specs=[
            pl.BlockSpec((tm, d), row),
            pl.BlockSpec((tm, d), row),
            pl.BlockSpec(wpqt.shape, fixed),
            pl.BlockSpec(sk.shape, lambda t, e: (0, 0, 0)),
            pl.BlockSpec((eb, d), lambda t, e: (e, 0)),
            pl.BlockSpec((d, eb), lambda t, e: (0, e)),
            pl.BlockSpec((1, d), fixed),
        ],
        out_specs=pl.BlockSpec((tm, d), row),
        out_shape=jax.ShapeDtypeStruct((n, d), F32),
        scratch_shapes=[head_scr, head_scr, head_scr, head_scr,
                        pltpu.VMEM((N_PEER_HEADS, 8, tm), F32), pltpu.VMEM((d, tm), F32)],
        compiler_params=_params("parallel", "arbitrary"),
        name="peer",
    )(hp, x1, wpqt, sk, eu, evt, gf)


def _split_w_in(w_in_l):
    a = 3 * ATTN_WIDTH + N_IDX_HEADS * IDX_DIM
    b = a + IDX_DIM + N_IDX_HEADS
    wm = jnp.concatenate([w_in_l[:, :a], w_in_l[:, b:]], axis=1).astype(BF16)
    ws = jnp.pad(w_in_l[:, a:b], ((0, 0), (0, LANES - (b - a)))).astype(BF16)
    return wm, ws


def _prompt_dsa_operands(q, k, v, qi, small, b, t, tq, tk):
    nq = t // tq
    qi_t = qi.reshape(b, nq, tq, N_IDX_HEADS, IDX_DIM).transpose(0, 1, 4, 3, 2)
    qi_t = qi_t.reshape(b, nq, IDX_DIM, N_IDX_HEADS * tq).astype(BF16)
    wi = small[:, IDX_DIM:IDX_DIM + N_IDX_HEADS]
    w_t = wi.reshape(b, nq, tq, N_IDX_HEADS).transpose(0, 1, 3, 2).reshape(b, nq, 1, N_IDX_HEADS * tq)
    ki = small[:, :IDX_DIM].reshape(b, t, IDX_DIM).astype(BF16)
    qt = q.astype(BF16).reshape(b, nq, tq, N_HEADS // 2, 2, HEAD_DIM).transpose(0, 1, 3, 4, 5, 2)
    eye = jnp.eye(2, dtype=BF16)
    qbd = qt[:, :, :, :, :, None, :] * eye[None, None, None, :, None, :, None]
    qbd = qbd.reshape(b, nq, N_HEADS // 2, 2 * HEAD_DIM, 2 * tq)
    kb = k.astype(BF16).reshape(b, t, ATTN_WIDTH)
    vt = v.astype(BF16).reshape(b, t // tk, tk, ATTN_WIDTH).transpose(0, 1, 3, 2)
    return qi_t, w_t, ki, qbd, kb, vt


def _sample_group_attention(qs, ks, vs, qis, smalls, page_table, cache_k_l, cache_v_l, cache_ki_l, bd, tn):
    past = page_table.shape[1] * PAGE_SIZE
    n_rows = N_HEADS * tn
    row_pad = ((0, 0), (0, LANES - n_rows), (0, 0))
    key_pad = ((0, 0), (0, 0), (0, PAGE_SIZE - tn))
    head_rows = lambda x: x.reshape(bd, tn, N_HEADS, -1).transpose(0, 2, 1, 3).reshape(bd, n_rows, -1)
    qi_rows = jnp.pad(head_rows(qis), row_pad).astype(BF16)
    w_col = jnp.pad(head_rows(smalls[:, IDX_DIM:IDX_DIM + N_IDX_HEADS]), row_pad)
    ki_new_t = jnp.pad(smalls[:, :IDX_DIM].reshape(bd, tn, IDX_DIM).transpose(0, 2, 1), key_pad)
    n_sel = min(MAX_TOPK, (past + tn) // 4)
    keys, keys_new, thr, need = _sample_score(page_table, qi_rows, w_col, ki_new_t,
                                              cache_ki_l.transpose(0, 1, 3, 2), n_sel, tn)
    q_bd = jnp.einsum('bqhd,hg->bhqgd', qs.reshape(bd, tn, N_HEADS, HEAD_DIM), jnp.eye(N_HEADS, dtype=F32))
    q_bd = jnp.pad(q_bd.reshape(bd, n_rows, ATTN_WIDTH), row_pad).astype(BF16)
    k_new_t = jnp.pad(ks.reshape(bd, tn, ATTN_WIDTH).transpose(0, 2, 1), key_pad)
    v_new_t = jnp.pad(vs.reshape(bd, tn, ATTN_WIDTH).transpose(0, 2, 1), key_pad)
    col = jnp.arange(ATTN_WIDTH)
    diag = (jnp.arange(LANES)[:, None] // tn == col[None, :] // HEAD_DIM).astype(F32)
    fold = (col[:, None] % HEAD_DIM == jnp.arange(HEAD_DIM)[None, :]).astype(BF16)
    earlier = (jnp.arange(PAGE_SIZE)[:, None] < jnp.arange(PAGE_SIZE)[None, :]).astype(BF16)
    tied = (jnp.max(need[:, :, LANES // 2:], axis=(1, 2)) > 0.5).astype(I32)
    out = _sample_attend(page_table, tied, q_bd, keys, keys_new, thr, need, k_new_t, v_new_t, diag, fold,
                         earlier, cache_k_l.transpose(0, 1, 3, 4, 2), cache_v_l.transpose(0, 1, 3, 4, 2))
    return out[:, :n_rows].reshape(bd, N_HEADS, tn, HEAD_DIM).transpose(0, 2, 1, 3).reshape(bd * tn, ATTN_WIDTH)


def _sample_gmlp_weights(w_sp_l, bias_p, tn):
    per_chunk = CHUNK // tn
    tril = jnp.tril(jnp.ones((tn, tn), dtype=bool))
    wsp_s = jnp.einsum('ij,gts->gitjs', jnp.eye(per_chunk, dtype=F32),
                       jnp.where(tril[None], w_sp_l[:, :tn, :tn], 0))
    wsp_s = wsp_s.reshape(N_GROUPS, CHUNK, CHUNK).astype(BF16)
    return wsp_s, jnp.tile(bias_p[:tn], (per_chunk, 1))


def kernel(x_prompt, x_sample, cache_k, cache_v, cache_kidx, page_table, norm_mix_g, w_in,
           gv_norm_g, w_sp, b_sp, w_out, norm_ffn_g, w_pq, peer_sub_keys, expert_u, expert_v,
           norm_final_g):
    b, t, d = x_prompt.shape
    bd, tn, _ = x_sample.shape
    depth = w_in.shape[0]
    assert depth == 1 and tn == 8 and t % 256 == 0 and (bd * tn) % CHUNK == 0
    n_pages = page_table.shape[1]
    past = n_pages * PAGE_SIZE
    l = 0

    wm, ws = _split_w_in(w_in[l])
    g_mix = norm_mix_g[l].reshape(1, d)
    gvg = gv_norm_g[l].reshape(1, GMLP_WIDTH)
    lane_group = jnp.arange(GMLP_WIDTH) // HEAD_DIM
    gmean = (lane_group[:, None] == lane_group[None, :]).astype(BF16) * (1.0 / HEAD_DIM)
    invf = ROPE_THETA ** (-jnp.arange(ROPE_HALF, dtype=F32) / ROPE_HALF)
    invf = jnp.tile(invf, LANES // ROPE_HALF).reshape(1, LANES)
    tril = jnp.tril(jnp.ones((CHUNK, CHUNK), dtype=bool))
    wsp_p = jnp.where(tril[None], w_sp[l], 0).astype(BF16)
    bias_p = jnp.repeat(b_sp[l].T, HEAD_DIM, axis=1)
    wsp_s, bias_s = _sample_gmlp_weights(w_sp[l], bias_p, tn)
    wo =w_out[l].astype(BF16)
    g_ffn = norm_ffn_g[l].reshape(1, d)
    wpqt = w_pq[l].T.astype(BF16)
    sk = peer_sub_keys[l].reshape(2 * N_PEER_HEADS, N_KEYS, PEER_HALF).astype(BF16)
    eu = expert_u[l].astype(BF16)
    evt = expert_v[l].T.astype(BF16)
    gf = norm_final_g.reshape(1, d)

    n_p = b * t
    pos_p = jnp.tile(jnp.arange(t, dtype=F32), b).reshape(n_p, 1)
    q, k, v, qi, u, gv, small = _project(x_prompt.reshape(n_p, d), pos_p, invf, g_mix, wm, ws,
                                         gvg, gmean, 256)
    tq, tk = 256, 256
    ops = _prompt_dsa_operands(q, k, v, qi, small, b, t, tq, tk)
    a_t = _dsa_prompt(*ops, tq, tk, min(MAX_TOPK, t // 4))
    a_p = a_t.transpose(0, 2, 1).reshape(n_p, ATTN_WIDTH).astype(BF16)
    x1_p, hp_p = _mix(a_p, u, gv, x_prompt.reshape(n_p, d), wsp_p, bias_p, wo, g_ffn)

    n_s = bd * tn
    pos_s = jnp.tile(past + jnp.arange(tn, dtype=F32), bd).reshape(n_s, 1)
    qs, ks, vs, qis, us, gvs, smalls = _project(x_sample.reshape(n_s, d), pos_s, invf, g_mix, wm,
                                                ws, gvg, gmean, n_s)
    a_s = _sample_group_attention(qs, ks, vs, qis, smalls, page_table, cache_k, cache_v,
                                  cache_kidx, bd, tn).astype(BF16)
    x1_s, hp_s = _mix(a_s, us, gvs, x_sample.reshape(n_s, d), wsp_s, bias_s, wo, g_ffn)

    x1 = jnp.concatenate([x1_p, x1_s], axis=0)
    hp = jnp.concatenate([hp_p, hp_s], axis=0)
    n_all = n_p + n_s
    tm = 640 if n_all % 640 == 0 else 128
    y = _peer(hp, x1, wpqt, sk, eu, evt, gf, tm, 1024)

    heads = (N_HEADS, HEAD_DIM)
    return (y[:n_p].reshape(b, t, d),
            y[n_p:].reshape(bd, tn, d),
            k.reshape(1, b, t, *heads),
            v.reshape(1, b, t, *heads),
            small[:, :IDX_DIM].reshape(1, b, t, IDX_DIM),
            ks.reshape(1, bd, tn, *heads),
            vs.reshape(1, bd, tn, *heads),
            smalls[:, :IDX_DIM].reshape(1, bd, tn, IDX_DIM),
            gvs.reshape(1, bd, tn, N_GROUPS, HEAD_DIM))
```

```python
import functools

import jax
import jax.numpy as jnp
from jax import lax
from jax.experimental import pallas as pl
from jax.experimental.pallas import tpu as pltpu

F32 = jnp.float32
BF16 = jnp.bfloat16
I32 = jnp.int32

EPS = 1e-6
ROPE_THETA = 10000.0
HEAD_DIM = 64
ROPE_HALF = HEAD_DIM // 2
N_HEADS = 8
ATTN_WIDTH = N_HEADS * HEAD_DIM
N_IDX_HEADS = 8
IDX_DIM = 64
MAX_TOPK = 256
N_GROUPS = 8
GMLP_WIDTH = N_GROUPS * HEAD_DIM
CHUNK = 128
PAGE_SIZE = 128
N_PEER_HEADS = 8
N_KEYS = 128
PEER_HALF = 64
PEER_TOPK = 16
LANES = 128

INT_MIN = -(2 ** 31)
NEG_BIG = -1e30
IDX_SCALE = (IDX_DIM ** -0.5) * (N_IDX_HEADS ** -0.5)
ATTN_SCALE = HEAD_DIM ** -0.5
LOG2E = 1.4426950408889634

VMEM_LIMIT_BYTES = 56 * 1024 * 1024


def _params(*sem):
    return pltpu.CompilerParams(dimension_semantics=sem, vmem_limit_bytes=VMEM_LIMIT_BYTES)


def _gelu(x):
    cdf = 0.5 * (1.0 + jnp.tanh(0.7978845608028654 * (x + 0.044715 * (x * x * x))))
    return x * cdf


def _rms(x, g):
    return x * lax.rsqrt(jnp.mean(x * x, axis=-1, keepdims=True) + EPS) * g


def _split3(x):
    a = x.astype(BF16)
    r = x - a.astype(F32)
    b = r.astype(BF16)
    c = (r - b.astype(F32)).astype(BF16)
    return a, b, c


def _dot_f32_by_bf16(x, m):
    a, b, c = _split3(x)
    d = lambda p: jnp.dot(p, m, preferred_element_type=F32)
    return d(a) + d(b) + d(c)


def _code_to_f32(code):
    bits = jnp.where(code < 0, code ^ 0x7FFFFFFF, code)
    return lax.bitcast_convert_type(bits, F32)


LOWEST_FINITE_CODE = -(2 ** 31) + 0x00800000


COUNT_ROWS = 64


def _count_partial(mask):
    rows, n = mask.shape
    ones = jnp.where(mask, 1.0, 0.0)
    return jnp.sum(ones.reshape(rows // COUNT_ROWS, COUNT_ROWS, n), axis=0)


def _kth_largest_key(count_ge, k, shape):
    c0 = count_ge(jnp.zeros(shape, I32))
    t0 = jnp.where(c0 >= k, 0, INT_MIN).astype(I32)

    def body(i, t):
        cand = t + lax.shift_left(jnp.int32(1), 30 - i)
        return jnp.where(count_ge(cand) >= k, cand, t)

    return lax.fori_loop(0, 31, body, t0)


def _second_cut(count_above_ge, k, shape):
    smallest_positive = jnp.ones(shape, I32)
    return lax.cond(jnp.max(count_above_ge(smallest_positive)) > 0.5,
                    lambda: _kth_largest_key(count_above_ge, k, shape),
                    lambda: jnp.zeros(shape, I32))


def _project_kernel(x_ref, pos_ref, invf_ref, g_ref, wm_ref, ws_ref, gvg_ref, gmean_ref,
                    q_ref, k_ref, v_ref, qi_ref, u_ref, gv_ref, small_ref):
    h = _rms(x_ref[...], g_ref[...]).astype(BF16)
    ang = pos_ref[...] * invf_ref[...]
    cos = jnp.cos(ang)
    sin = jnp.sin(ang)
    lane = lax.broadcasted_iota(I32, ang.shape, 1)
    first = (lane % HEAD_DIM) < ROPE_HALF
    sin_signed = jnp.where(first, -sin, sin)

    def rope(xc):
        partner = jnp.where(first, pltpu.roll(xc, LANES - ROPE_HALF, 1), pltpu.roll(xc, ROPE_HALF, 1))
        return xc * cos + partner * sin_signed

    def proj(c0):
        return jnp.dot(h, wm_ref[:, c0:c0 + ATTN_WIDTH], preferred_element_type=F32)

    for ref, c0 in ((q_ref, 0), (k_ref, ATTN_WIDTH), (qi_ref, 3 * ATTN_WIDTH)):
        p = proj(c0)
        for c in range(ATTN_WIDTH // LANES):
            ref[:, c * LANES:(c + 1) * LANES] = rope(p[:, c * LANES:(c + 1) * LANES])
    v_ref[...] = proj(2 * ATTN_WIDTH)
    u_ref[...] = _gelu(proj(4 * ATTN_WIDTH))
    gl = _gelu(proj(5 * ATTN_WIDTH))
    ms = _dot_f32_by_bf16(gl * gl, gmean_ref[...])
    gv_ref[...] = gl * lax.rsqrt(ms + EPS) * gvg_ref[...]
    sm = jnp.dot(h, ws_ref[...], preferred_element_type=F32)
    small_ref[...] = jnp.where(lane < IDX_DIM, rope(sm), sm)


def _project(x2d, pos, invf, g, wm, ws, gvg, gmean, tm):
    n, d = x2d.shape
    row = lambda i: (i, 0)
    fixed = lambda i: (0, 0)
    wide = jax.ShapeDtypeStruct((n, ATTN_WIDTH), F32)
    return pl.pallas_call(
        _project_kernel,
        grid=(n // tm,),
        in_specs=[
            pl.BlockSpec((tm, d), row),
            pl.BlockSpec((tm, 1), row),
            pl.BlockSpec((1, LANES), fixed),
            pl.BlockSpec((1, d), fixed),
            pl.BlockSpec(wm.shape, fixed),
            pl.BlockSpec(ws.shape, fixed),
            pl.BlockSpec((1, GMLP_WIDTH), fixed),
            pl.BlockSpec(gmean.shape, fixed),
        ],
        out_specs=[pl.BlockSpec((tm, ATTN_WIDTH), row)] * 6 + [pl.BlockSpec((tm, LANES), row)],
        out_shape=[wide] * 6 + [jax.ShapeDtypeStruct((n, LANES), F32)],
        compiler_params=_params("parallel"),
        name="project",
    )(x2d, pos, invf, g, wm, ws, gvg, gmean)


def _dsa_prompt_kernel(qi_ref, w_ref, ki_ref, qbd_ref, k_ref, vt_ref, out_ref, keys_ref, acc_ref, lg_ref,
                       *, tq, tk, n_sel):
    j = pl.program_id(1)
    n_tiles = pl.num_programs(1)
    sub = tq // tk
    nkb = (j + 1) * sub
    q0 = j * tq
    qi = qi_ref[0, 0]
    w = w_ref[0, 0] * IDX_SCALE
    qpos = q0 + lax.broadcasted_iota(I32, (tk, tq), 1)
    krow = lax.broadcasted_iota(I32, (tk, tq), 0)

    def score_body(kq, carry):
        for u in range(sub):
            r0 = pl.multiple_of(kq * tq, tq) + u * tk
            d = jnp.dot(ki_ref[0, pl.ds(r0, tk), :], qi, preferred_element_type=F32)
            s = jnp.zeros((tk, tq), F32)
            for h in range(N_IDX_HEADS):
                s = s + jnp.maximum(d[:, h * tq:(h + 1) * tq], 0.0) * w[:, h * tq:(h + 1) * tq]
            keys_ref[pl.ds(r0, tk), :] = jnp.where(krow + r0 <= qpos, s, -jnp.inf)
        return carry

    lax.fori_loop(0, j + 1, score_body, 0)

    @pl.when(jnp.logical_and((j + 1) % 2 == 1, j + 1 < n_tiles))
    def _blank():
        keys_ref[pl.ds(pl.multiple_of((j + 1) * tq, tq), tq), :] = jnp.full((tq, tq), -jnp.inf, F32)

    def count_ge(code):
        cand = _code_to_f32(code)

        def body(kb, acc):
            blk = keys_ref[pl.ds(pl.multiple_of(kb * 2 * tq, 2 * tq), 2 * tq), :]
            return acc + _count_partial(blk >= cand)
        acc = lax.fori_loop(0, (j + 2) // 2, body, jnp.zeros((COUNT_ROWS, tq), F32))
        return jnp.sum(acc, axis=0, keepdims=True)

    cut = jnp.maximum(_kth_largest_key(count_ge, n_sel, (1, tq)), LOWEST_FINITE_CODE)
    thr = _code_to_f32(cut)

    @pl.when(jnp.max(count_ge(cut)) > n_sel + 0.5)
    def _break_ties():
        blk_rows = 2 * tq
        n_blocks = (j + 2) // 2
        thr_next = _code_to_f32(cut + 1)
        need = n_sel - count_ge(cut + 1)

        def load(kb):
            rows = pl.ds(pl.multiple_of(kb * blk_rows, blk_rows), blk_rows)
            blk = keys_ref[rows, :]
            above = jnp.where(blk >= thr, jnp.where(blk < thr_next, blk - thr, -jnp.inf), -jnp.inf)
            return rows, blk, above

        def count_above_ge(code):
            cand = _code_to_f32(code)

            def body(kb, acc):
                return acc + _count_partial(load(kb)[2] >= cand)
            acc = lax.fori_loop(0, n_blocks, body, jnp.zeros((COUNT_ROWS, tq), F32))
            return jnp.sum(acc, axis=0, keepdims=True)

        cut2 = _second_cut(count_above_ge, need, (1, tq))
        thr2 = _code_to_f32(cut2)
        need2 = need - count_above_ge(cut2 + 1)
        earlier = (lax.broadcasted_iota(I32, (blk_rows, blk_rows), 1)
                   < lax.broadcasted_iota(I32, (blk_rows, blk_rows), 0))
        earlier = jnp.where(earlier, 1.0, 0.0).astype(BF16)

        def body(kb, seen):
            rows, blk, above = load(kb)
            eq = jnp.where(above == thr2, 1.0, 0.0)
            before = jnp.dot(earlier, eq.astype(BF16), preferred_element_type=F32) + seen
            late = eq * before >= jnp.maximum(need2, 0.5)
            keys_ref[rows, :] = jnp.where(above < thr2, jnp.where(above >= 0.0, -jnp.inf, blk),
                                          jnp.where(late, -jnp.inf, blk))
            return seen + jnp.sum(eq, axis=0, keepdims=True)

        lax.fori_loop(0, n_blocks, body, jnp.zeros((1, tq), F32))

    n_pairs = N_HEADS // 2
    acc_ref[...] = jnp.zeros(acc_ref.shape, F32)

    def logits(kb):
        r0 = pl.multiple_of(kb * tk, tk)
        return [jnp.dot(k_ref[0, pl.ds(r0, tk), p * LANES:(p + 1) * LANES], qbd_ref[0, 0, p],
                        preferred_element_type=F32) for p in range(n_pairs)]

    def stash(slot, lgs):
        for p in range(n_pairs):
            lg_ref[slot, :, p * 2 * tq:(p + 1) * 2 * tq] = lgs[p]

    stash(0, logits(0))

    def att_body(kb, carry):
        ms, ls = carry
        r0 = pl.multiple_of(kb * tk, tk)
        slot = kb % 2
        nxt = logits(jnp.minimum(kb + 1, nkb - 1))
        sel = keys_ref[pl.ds(r0, tk), :] >= thr
        new_ms, new_ls = [], []
        for p in range(n_pairs):
            rows = slice(p * LANES, (p + 1) * LANES)
            lg = lg_ref[slot, :, p * 2 * tq:(p + 1) * 2 * tq] * (ATTN_SCALE * LOG2E)
            lm = jnp.concatenate([jnp.where(sel, lg[:, :tq], NEG_BIG),
                                  jnp.where(sel, lg[:, tq:], NEG_BIG)], axis=1)
            m_new = jnp.maximum(ms[p], jnp.max(lm, axis=0, keepdims=True))
            pe = jnp.exp2(lm - m_new)
            alpha = jnp.exp2(ms[p] - m_new)
            new_ls.append(alpha * ls[p] + jnp.sum(pe, axis=0, keepdims=True))
            new_ms.append(m_new)
            pb = pe.astype(BF16)
            vtb = vt_ref[0, kb, rows, :]
            pv0 = jnp.dot(vtb[:HEAD_DIM], pb[:, :tq], preferred_element_type=F32)
            pv1 = jnp.dot(vtb[HEAD_DIM:], pb[:, tq:], preferred_element_type=F32)
            acc = acc_ref[rows, :]
            acc_ref[rows, :] = jnp.concatenate([alpha[:, :tq] * acc[:HEAD_DIM] + pv0,
                                                alpha[:, tq:] * acc[HEAD_DIM:] + pv1], axis=0)
        stash(1 - slot, nxt)
        return tuple(new_ms), tuple(new_ls)

    init = (tuple(jnp.full((1, 2 * tq), NEG_BIG, F32) for _ in range(n_pairs)),
            tuple(jnp.zeros((1, 2 * tq), F32) for _ in range(n_pairs)))
    _, ls = lax.fori_loop(0, nkb, att_body, init)
    for p in range(n_pairs):
        rows = slice(p * LANES, (p + 1) * LANES)
        acc = acc_ref[rows, :]
        out_ref[0, rows, :] = jnp.concatenate(
            [acc[:HEAD_DIM] / ls[p][:, :tq], acc[HEAD_DIM:] / ls[p][:, tq:]], axis=0)


def _dsa_prompt(qi_t, w_t, ki, qbd, k, vt, tq, tk, n_sel):
    b, nq = qi_t.shape[:2]
    t = k.shape[1]
    tile = lambda bb, j: (bb, j, 0, 0)
    whole3 = lambda bb, j: (bb, 0, 0)
    return pl.pallas_call(
        functools.partial(_dsa_prompt_kernel, tq=tq, tk=tk, n_sel=n_sel),
        grid=(b, nq),
        in_specs=[
            pl.BlockSpec((1, 1, IDX_DIM, N_IDX_HEADS * tq), tile),
            pl.BlockSpec((1, 1, 1, N_IDX_HEADS * tq), tile),
            pl.BlockSpec((1, t, IDX_DIM), whole3, pipeline_mode=pl.Buffered(1)),
            pl.BlockSpec((1, 1, N_HEADS // 2, LANES, 2 * tq), lambda bb, j: (bb, j, 0, 0, 0)),
            pl.BlockSpec((1, t, ATTN_WIDTH), whole3, pipeline_mode=pl.Buffered(1)),
            pl.BlockSpec((1, t // tk, ATTN_WIDTH, tk), lambda bb, j: (bb, 0, 0, 0),
                         pipeline_mode=pl.Buffered(1)),
        ],
        out_specs=pl.BlockSpec((1, ATTN_WIDTH, tq), lambda bb, j: (bb, 0, j)),
        out_shape=jax.ShapeDtypeStruct((b, ATTN_WIDTH, t), F32),
        scratch_shapes=[pltpu.VMEM((t, tq), F32), pltpu.VMEM((ATTN_WIDTH, tq), F32),
                        pltpu.VMEM((2, tk, N_HEADS * tq), F32)],
        compiler_params=_params("parallel", "parallel"),
        name="dsa_prompt",
    )(qi_t, w_t, ki, qbd, k, vt)


PAGES_PER_STEP = 8
STEP_KEYS = PAGES_PER_STEP * PAGE_SIZE


def _sample_score_kernel(pt_ref, qi_ref, w_ref, kinew_ref, *rest, n_sel, n_new):
    pages = rest[:PAGES_PER_STEP]
    keys_ref, keysnew_ref, thr_ref, need_ref, scr = rest[PAGES_PER_STEP:]
    s = pl.program_id(1)
    n_steps = pl.num_programs(1)
    qi = qi_ref[...]
    w = w_ref[...] * IDX_SCALE

    def score(ki_t):
        d = jnp.dot(qi, ki_t.astype(BF16), preferred_element_type=F32)
        sc = jnp.maximum(d, 0.0) * w
        tot = jnp.zeros((n_new, ki_t.shape[1]), F32)
        for h in range(N_IDX_HEADS):
            tot = tot + sc[h * n_new:(h + 1) * n_new]
        return tot

    key = score(jnp.concatenate([page[...] for page in pages], axis=1))
    keys_ref[...] = key
    scr[s] = key

    @pl.when(s == n_steps - 1)
    def _finish():
        key = score(kinew_ref[...])
        qrow = lax.broadcasted_iota(I32, key.shape, 0)
        kcol = lax.broadcasted_iota(I32, key.shape, 1)
        key = jnp.where(kcol <= qrow, key, -jnp.inf)
        keysnew_ref[...] = key
        scr[n_steps] = jnp.concatenate(
            [key, jnp.full((n_new, STEP_KEYS - PAGE_SIZE), -jnp.inf, F32)], axis=1)

        def count_ge(code):
            cand = _code_to_f32(code)
            acc = jnp.zeros((n_new, STEP_KEYS), F32)
            for slab in range(scr.shape[0]):
                acc = acc + jnp.where(scr[slab] >= cand, 1.0, 0.0)
            return jnp.sum(acc, axis=1, keepdims=True)

        cut = jnp.maximum(_kth_largest_key(count_ge, n_sel, (n_new, 1)), LOWEST_FINITE_CODE)
        thr = _code_to_f32(cut)
        thr_next = _code_to_f32(cut + 1)
        need = n_sel - count_ge(cut + 1)

        def count_above_ge(code):
            cand = _code_to_f32(code)
            acc = jnp.zeros((n_new, STEP_KEYS), F32)
            for slab in range(scr.shape[0]):
                sc = scr[slab]
                above = jnp.where(sc >= thr, jnp.where(sc < thr_next, sc - thr, -jnp.inf), -jnp.inf)
                acc = acc + jnp.where(above >= cand, 1.0, 0.0)
            return jnp.sum(acc, axis=1, keepdims=True)

        cut2 = _second_cut(count_above_ge, need, (n_new, 1))
        need2 = need - count_above_ge(cut2 + 1)
        surplus = count_ge(cut) - n_sel
        lane = lax.broadcasted_iota(I32, thr_ref.shape, 1)
        wide = lambda x: jnp.broadcast_to(x, thr_ref.shape)
        thr_ref[...] = jnp.where(lane < LANES // 2, wide(thr), wide(thr_next))
        need_ref[...] = jnp.where(lane < LANES // 4, wide(_code_to_f32(cut2)),
                                  jnp.where(lane < LANES // 2, wide(need2), wide(surplus)))


def _page_spec(tail, k):
    zeros = (0,) * len(tail)
    return pl.BlockSpec((None, None) + tail, lambda b, s, pt, *_: (0, pt[b, s * PAGES_PER_STEP + k]) + zeros)


def _sample_score(page_table, qi_rows, w_col, ki_new, cache_ki, n_sel, n_new):
    nb, n_pages = page_table.shape
    n_steps = n_pages // PAGES_PER_STEP
    per_b = lambda b, s, pt: (b, 0, 0)
    grid_spec = pltpu.PrefetchScalarGridSpec(
        num_scalar_prefetch=1,
        grid=(nb, n_steps),
        in_specs=[
            pl.BlockSpec((None, LANES, IDX_DIM), per_b),
            pl.BlockSpec((None, LANES, 1), per_b),
            pl.BlockSpec((None, IDX_DIM, PAGE_SIZE), per_b),
        ] + [_page_spec((IDX_DIM, PAGE_SIZE), k) for k in range(PAGES_PER_STEP)],
        out_specs=[
            pl.BlockSpec((None, n_new, STEP_KEYS), lambda b, s, pt: (b, 0, s)),
            pl.BlockSpec((None, n_new, PAGE_SIZE), per_b),
            pl.BlockSpec((None, n_new, LANES), per_b),
            pl.BlockSpec((None, n_new, LANES), per_b),
        ],
        scratch_shapes=[pltpu.VMEM((n_steps + 1, n_new, STEP_KEYS), F32)],
    )
    return pl.pallas_call(
        functools.partial(_sample_score_kernel, n_sel=n_sel, n_new=n_new),
        grid_spec=grid_spec,
        out_shape=[
            jax.ShapeDtypeStruct((nb, n_new, n_pages * PAGE_SIZE), F32),
            jax.ShapeDtypeStruct((nb, n_new, PAGE_SIZE), F32),
            jax.ShapeDtypeStruct((nb, n_new, LANES), F32),
            jax.ShapeDtypeStruct((nb, n_new, LANES), F32),
        ],
        compiler_params=_params("parallel", "arbitrary"),
        name="sample_score",
    )(page_table, qi_rows, w_col, ki_new, *([cache_ki] * PAGES_PER_STEP))


def _sample_attend_kernel(pt_ref, tied_ref, q_ref, keys_ref, keysnew_ref, thr_ref, need_ref, knew_ref, vnew_ref,
                          diag_ref, fold_ref, earlier_ref, *rest):
    kpages = rest[:PAGES_PER_STEP]
    vpages = rest[PAGES_PER_STEP:2 * PAGES_PER_STEP]
    out_ref, m_ref, l_ref, acc_ref, seen_ref = rest[2 * PAGES_PER_STEP:]
    s = pl.program_id(1)
    n_steps = pl.num_programs(1)
    n_new = keys_ref.shape[0]
    nt = (((1,), (1,)), ((), ()))

    @pl.when(s == 0)
    def _init():
        m_ref[...] = jnp.full(m_ref.shape, NEG_BIG, F32)
        l_ref[...] = jnp.zeros(l_ref.shape, F32)
        acc_ref[...] = jnp.zeros(acc_ref.shape, F32)
        seen_ref[...] = jnp.zeros(seen_ref.shape, F32)

    q = q_ref[...]
    thr = thr_ref[:, 0:1]
    thr_next = thr_ref[:, LANES // 2:LANES // 2 + 1]
    thr2 = need_ref[:, 0:1]
    need2 = need_ref[:, LANES // 4:LANES // 4 + 1]

    tied = tied_ref[pl.program_id(0)] != 0

    def selected_in_order(keyblk):
        above = jnp.where(keyblk >= thr, jnp.where(keyblk < thr_next, keyblk - thr, -jnp.inf), -jnp.inf)
        eq = jnp.where(above == thr2, 1.0, 0.0)
        seen = seen_ref[...]
        before = jnp.dot(eq.astype(BF16), earlier_ref[...], preferred_element_type=F32) + seen
        seen_ref[...] = seen + jnp.sum(eq, axis=1, keepdims=True)
        return jnp.where(keyblk >= thr_next, 1.0,
                         jnp.where(above > thr2, 1.0, jnp.where(before < need2, eq, 0.0)))

    def selected(keys):
        def in_order(kk):
            return jnp.concatenate([selected_in_order(kk[:, i:i + PAGE_SIZE])
                                    for i in range(0, kk.shape[1], PAGE_SIZE)], axis=1)
        return lax.cond(tied, in_order, lambda kk: jnp.where(kk >= thr, 1.0, 0.0), keys)

    def accumulate(k_t, v_t, chosen):
        lg = jnp.dot(q, k_t.astype(BF16), preferred_element_type=F32) * (ATTN_SCALE * LOG2E)
        sel = jnp.concatenate([chosen] * (LANES // n_new), axis=0) > 0.5
        lm = jnp.where(sel, lg, NEG_BIG)
        m = m_ref[...]
        m_new = jnp.maximum(m, jnp.max(lm, axis=1, keepdims=True))
        pe = jnp.exp2(lm - m_new)
        alpha = jnp.exp2(m - m_new)
        l_ref[...] = alpha * l_ref[...] + jnp.sum(pe, axis=1, keepdims=True)
        pv = lax.dot_general(pe.astype(BF16), v_t.astype(BF16), nt, preferred_element_type=F32)
        acc_ref[...] = alpha * acc_ref[...] + pv
        m_ref[...] = m_new

    flat = (ATTN_WIDTH, PAGE_SIZE)
    accumulate(jnp.concatenate([p[...].reshape(flat) for p in kpages], axis=1),
               jnp.concatenate([p[...].reshape(flat) for p in vpages], axis=1),
               selected(keys_ref[...]))

    @pl.when(s == n_steps - 1)
    def _finish():
        accumulate(knew_ref[...], vnew_ref[...], selected(keysnew_ref[...]))
        own_head = acc_ref[...] / l_ref[...] * diag_ref[...]
        out_ref[...] = _dot_f32_by_bf16(own_head, fold_ref[...])


def _sample_attend(page_table, tied, q_rows, keys, keys_new, thr, need, k_new, v_new, diag, fold, earlier,
                   cache_k_t, cache_v_t):
    nb, n_pages = page_table.shape
    n_steps = n_pages // PAGES_PER_STEP
    n_new = keys.shape[1]
    per_b = lambda b, s, pt, tied: (b, 0, 0)
    fixed = lambda b, s, pt, tied: (0, 0)
    page = (N_HEADS, HEAD_DIM, PAGE_SIZE)
    grid_spec = pltpu.PrefetchScalarGridSpec(
        num_scalar_prefetch=2,
        grid=(nb, n_steps),
        in_specs=[
            pl.BlockSpec((None, LANES, ATTN_WIDTH), per_b),
            pl.BlockSpec((None, n_new, STEP_KEYS), lambda b, s, pt, tied: (b, 0, s)),
            pl.BlockSpec((None, n_new, PAGE_SIZE), per_b),
            pl.BlockSpec((None, n_new, LANES), per_b),
            pl.BlockSpec((None, n_new, LANES), per_b),
            pl.BlockSpec((None, ATTN_WIDTH, PAGE_SIZE), per_b),
            pl.BlockSpec((None, ATTN_WIDTH, PAGE_SIZE), per_b),
            pl.BlockSpec(diag.shape, fixed),
            pl.BlockSpec(fold.shape, fixed),
            pl.BlockSpec(earlier.shape, fixed),
        ] + [_page_spec(page, k) for k in range(PAGES_PER_STEP)] * 2,
        out_specs=pl.BlockSpec((None, LANES, HEAD_DIM), per_b),
        scratch_shapes=[pltpu.VMEM((LANES, 1), F32), pltpu.VMEM((LANES, 1), F32),
                        pltpu.VMEM((LANES, ATTN_WIDTH), F32), pltpu.VMEM((n_new, 1), F32)],
    )
    return pl.pallas_call(
        _sample_attend_kernel,
        grid_spec=grid_spec,
        out_shape=jax.ShapeDtypeStruct((nb, LANES, HEAD_DIM), F32),
        compiler_params=_params("parallel", "arbitrary"),
        name="sample_attend",
    )(page_table, tied, q_rows, keys, keys_new, thr, need, k_new, v_new, diag, fold, earlier,
      *([cache_k_t] * PAGES_PER_STEP), *([cache_v_t] * PAGES_PER_STEP))


def _mix_kernel(a_ref, u_ref, gv_ref, x_ref, wsp_ref, bias_ref, wo_ref, g_ref, x1_ref, hp_ref):
    gvb = gv_ref[...].astype(BF16)
    group = lax.broadcasted_iota(I32, gvb.shape, 1) // HEAD_DIM
    mixed = bias_ref[...]
    for g in range(N_GROUPS):
        r = jnp.dot(wsp_ref[g], gvb, preferred_element_type=F32)
        mixed = mixed + jnp.where(group == g, r, 0.0)
    gm = (u_ref[...] * mixed).astype(BF16)
    mix = (jnp.dot(a_ref[...], wo_ref[:ATTN_WIDTH, :], preferred_element_type=F32)
           + jnp.dot(gm, wo_ref[ATTN_WIDTH:, :], preferred_element_type=F32))
    x1 = x_ref[...] + mix
    x1_ref[...] = x1
    hp_ref[...] = _rms(x1, g_ref[...]).astype(BF16)


def _mix(a, u, gv, x, wsp, bias, wo, g):
    n, d = x.shape
    row = lambda i: (i, 0)
    fixed = lambda i: (0, 0)
    return pl.pallas_call(
        _mix_kernel,
        grid=(n // CHUNK,),
        in_specs=[
            pl.BlockSpec((CHUNK, ATTN_WIDTH), row),
            pl.BlockSpec((CHUNK, GMLP_WIDTH), row),
            pl.BlockSpec((CHUNK, GMLP_WIDTH), row),
            pl.BlockSpec((CHUNK, d), row),
            pl.BlockSpec(wsp.shape, lambda i: (0, 0, 0)),
            pl.BlockSpec(bias.shape, fixed),
            pl.BlockSpec(wo.shape, fixed),
            pl.BlockSpec((1, d), fixed),
        ],
        out_specs=[pl.BlockSpec((CHUNK, d), row), pl.BlockSpec((CHUNK, d), row)],
        out_shape=[jax.ShapeDtypeStruct((n, d), F32), jax.ShapeDtypeStruct((n, d), BF16)],
        compiler_params=_params("parallel"),
        name="mix",
    )(a, u, gv, x, wsp, bias, wo, g)


def _col_reduce(x, op):
    slabs = [x[i:i + 8] for i in range(0, x.shape[0], 8)]
    while len(slabs) > 1:
        nxt = [op(slabs[i], slabs[i + 1]) for i in range(0, len(slabs) - 1, 2)]
        if len(slabs) % 2:
            nxt.append(slabs[-1])
        slabs = nxt
    red = jnp.max if op is jnp.maximum else jnp.min
    return red(slabs[0], axis=0, keepdims=True)


def _extract16(arrays):
    rows, n = arrays[0].shape
    ridx = lax.broadcasted_iota(I32, (rows, n), 0).astype(F32)
    r16 = lax.broadcasted_iota(I32, (PEER_TOPK, n), 0)

    def body(r, carry):
        out = []
        for cur, vals, rank in carry:
            mx = _col_reduce(cur, jnp.maximum)
            hit = ridx == _col_reduce(jnp.where(cur == mx, ridx, float(rows)), jnp.minimum)
            out.append((jnp.where(hit, -jnp.inf, cur), jnp.where(r16 == r, mx, vals),
                        jnp.where(hit, lax.convert_element_type(r, F32), rank)))
        return tuple(out)

    init = tuple((s, jnp.zeros((PEER_TOPK, n), F32), jnp.full((rows, n), float(PEER_TOPK), F32))
                 for s in arrays)
    res = lax.fori_loop(0, PEER_TOPK, body, init)
    return [(vals, rank) for _, vals, rank in res]


def _batcher_pairs(n):
    pairs = []

    def merge(lo, hi, r):
        step = r * 2
        if step < hi - lo:
            merge(lo, hi, step)
            merge(lo + r, hi, step)
            pairs.extend((i, i + r) for i in range(lo + r, hi - r, step))
        else:
            pairs.append((lo, lo + r))

    def sort(lo, hi):
        if hi - lo >= 1:
            mid = lo + (hi - lo) // 2
            sort(lo, mid)
            sort(mid + 1, hi)
            merge(lo, hi, 1)

    sort(0, n - 1)
    return pairs


_SORT16 = _batcher_pairs(PEER_TOPK)
SUBLANES = 8


def _compare_exchange(v, i, j):
    v[i], v[j] = jnp.maximum(v[i], v[j]), jnp.minimum(v[i], v[j])


def _merge_top16(a, b):
    v = [jnp.maximum(a[i], b[PEER_TOPK - 1 - i]) for i in range(PEER_TOPK)]
    d = PEER_TOPK // 2
    while d:
        for i in range(PEER_TOPK):
            if not i & d:
                _compare_exchange(v, i, i + d)
        d //= 2
    return v


def _fold_sublanes(v):
    for sh in (4, 2, 1):
        v = _merge_top16(v, [pltpu.roll(x, sh, 0) for x in v])
    return v


def _top16_values(x):
    v = [x[i * SUBLANES:(i + 1) * SUBLANES] for i in range(x.shape[0] // SUBLANES)]
    for i, j in _SORT16:
        _compare_exchange(v, i, j)
    return _fold_sublanes(v)


def _rows(slab, n_slabs):
    return jnp.concatenate([slab] * n_slabs, axis=0)


def _peer_gates_fast(s1, s2):
    n_slabs = s1.shape[0] // SUBLANES
    a = _top16_values(s1)
    b = _top16_values(s2)
    sub = lax.broadcasted_iota(I32, a[0].shape, 0)

    def spread(vals, g):
        out = vals[g * SUBLANES + SUBLANES - 1]
        for t in range(SUBLANES - 2, -1, -1):
            out = jnp.where(sub == t, vals[g * SUBLANES + t], out)
        return out

    b_lo, b_hi = spread(b, 0), spread(b, 1)
    cands = [x + b_lo for x in a] + [x + b_hi for x in a]
    best = _fold_sublanes(_merge_top16(cands[:PEER_TOPK], cands[PEER_TOPK:]))
    tau = best[PEER_TOPK - 1]
    z = functools.reduce(jnp.add, [jnp.exp(x - best[0]) for x in best])
    in1 = s1 >= _rows(a[PEER_TOPK - 1], n_slabs)
    in2 = s2 >= _rows(b[PEER_TOPK - 1], n_slabs)
    e1 = jnp.where(in1, jnp.exp(s1 - _rows(a[0], n_slabs)), 0.0)
    e2 = jnp.where(in2, jnp.exp(s2 - _rows(b[0], n_slabs)), 0.0) / _rows(z, n_slabs)
    count = lambda m: jnp.sum(jnp.where(m, 1.0, 0.0), axis=0, keepdims=True)
    n_pairs = count(jnp.concatenate(cands, axis=0) >= _rows(tau, 2 * PEER_TOPK))
    tied = jnp.max(jnp.maximum(jnp.maximum(count(in1), count(in2)), n_pairs)) > PEER_TOPK + 0.5
    return e1, e2, tau, tied


def _peer_gates_exact(s1, s2):
    n = s1.shape[1]
    (a, k1), (b, k2) = _extract16([s1, s2])
    cand = jnp.concatenate([a[r:r + 1, :] + b for r in range(PEER_TOPK)], axis=0)
    (best, kc), = _extract16([cand])
    z = jnp.sum(jnp.exp(best - best[0:1, :]), axis=0, keepdims=True)
    e1 = jnp.where(k1 < PEER_TOPK, jnp.exp(s1 - a[0:1, :]), 0.0)
    e2 = jnp.where(k2 < PEER_TOPK, jnp.exp(s2 - b[0:1, :]), 0.0) / z
    taken = jnp.where(kc < PEER_TOPK, 1.0, 0.0)
    r1 = jnp.zeros_like(s1)
    for r in range(PEER_TOPK):
        row_len = jnp.sum(taken[r * PEER_TOPK:(r + 1) * PEER_TOPK], axis=0, keepdims=True)
        r1 = r1 + jnp.where(k1 == r, row_len, 0.0)
    r2 = jnp.where(k2 < PEER_TOPK, -k2, -4.0 * PEER_TOPK)
    return e1, e2, r1, r2, jnp.full((SUBLANES, n), 0.5, F32)


def _peer_kernel(hp_ref, x1_ref, wpqt_ref, sk_ref, u_ref, vt_ref, gf_ref, y_ref,
                 s1_ref, s2_ref, e1_ref, e2_ref, tau_ref, out_ref, *, tm, eb):
    e = pl.program_id(1)
    n_e = pl.num_programs(1)
    hp = hp_ref[...]
    nt = (((1,), (1,)), ((), ()))

    @pl.when(e == 0)
    def _select():
        qt = lax.dot_general(wpqt_ref[...], hp, nt, preferred_element_type=F32).astype(BF16)
        for hh in range(N_PEER_HEADS):
            for c, ref in enumerate((s1_ref, s2_ref)):
                idx = hh * 2 + c
                ref[hh] = jnp.dot(sk_ref[idx], qt[idx * PEER_HALF:(idx + 1) * PEER_HALF, :],
                                  preferred_element_type=F32)

        def head_body(hh, carry):
            for ch in range(tm // LANES):
                sl = slice(ch * LANES, (ch + 1) * LANES)
                s1 = s1_ref[hh, :, sl]
                s2 = s2_ref[hh, :, sl]
                e1, e2, tau, tied = _peer_gates_fast(s1, s2)
                e1_ref[hh, :, sl] = e1
                e2_ref[hh, :, sl] = e2
                tau_ref[hh, :, sl] = tau

                @pl.when(tied)
                def _redo(hh=hh, sl=sl, s1=s1, s2=s2):
                    e1x, e2x, r1, r2, cut = _peer_gates_exact(s1, s2)
                    e1_ref[hh, :, sl] = e1x
                    e2_ref[hh, :, sl] = e2x
                    s1_ref[hh, :, sl] = r1
                    s2_ref[hh, :, sl] = r2
                    tau_ref[hh, :, sl] = cut
            return carry

        lax.fori_loop(0, N_PEER_HEADS, head_body, 0)
        out_ref[...] = jnp.zeros(out_ref.shape, F32)

    act = _gelu(lax.dot_general(u_ref[...], hp, nt, preferred_element_type=F32))
    parts = []
    for ii in range(eb // N_KEYS):
        i = e * (eb // N_KEYS) + ii
        wt = jnp.zeros((N_KEYS, tm), F32)
        for hh in range(N_PEER_HEADS):
            cs = s1_ref[hh, pl.ds(i, 1), :] + s2_ref[hh]
            wt = wt + jnp.where(cs >= tau_ref[hh, 0:1, :],
                                e1_ref[hh, pl.ds(i, 1), :] * e2_ref[hh], 0.0)
        parts.append((wt * act[ii * N_KEYS:(ii + 1) * N_KEYS, :]).astype(BF16))
    gt = jnp.concatenate(parts, axis=0)
    out_ref[...] += jnp.dot(vt_ref[...], gt, preferred_element_type=F32)

    @pl.when(e == n_e - 1)
    def _finish():
        y_ref[...] = _rms(x1_ref[...] + out_ref[...].T, gf_ref[...])


def _peer(hp, x1, wpqt, sk, eu, evt, gf, tm, eb):
    n, d = x1.shape
    n_exp = eu.shape[0]
    row = lambda t, e: (t, 0)
    fixed = lambda t, e: (0, 0)
    head_scr = pltpu.VMEM((N_PEER_HEADS, N_KEYS, tm), F32)
    return pl.pallas_call(
        functools.partial(_peer_kernel, tm=tm, eb=eb),
        grid=(n // tm, n_exp // eb),
        in_specs=[
            pl.BlockSpec((tm, d), row),
            pl.BlockSpec((tm, d), row),
            pl.BlockSpec(wpqt.shape, fixed),
            pl.BlockSpec(sk.shape, lambda t, e: (0, 0, 0)),
            pl.BlockSpec((eb, d), lambda t, e: (e, 0)),
            pl.BlockSpec((d, eb), lambda t, e: (0, e)),
            pl.BlockSpec((1, d), fixed),
        ],
        out_specs=pl.BlockSpec((tm, d), row),
        out_shape=jax.ShapeDtypeStruct((n, d), F32),
        scratch_shapes=[head_scr, head_scr, head_scr, head_scr,
                        pltpu.VMEM((N_PEER_HEADS, 8, tm), F32), pltpu.VMEM((d, tm), F32)],
        compiler_params=_params("parallel", "arbitrary"),
        name="peer",
    )(hp, x1, wpqt, sk, eu, evt, gf)


def _split_w_in(w_in_l):
    a = 3 * ATTN_WIDTH + N_IDX_HEADS * IDX_DIM
    b = a + IDX_DIM + N_IDX_HEADS
    wm = jnp.concatenate([w_in_l[:, :a], w_in_l[:, b:]], axis=1).astype(BF16)
    ws = jnp.pad(w_in_l[:, a:b], ((0, 0), (0, LANES - (b - a)))).astype(BF16)
    return wm, ws


def _prompt_dsa_operands(q, k, v, qi, small, b, t, tq, tk):
    nq = t // tq
    qi_t = qi.reshape(b, nq, tq, N_IDX_HEADS, IDX_DIM).transpose(0, 1, 4, 3, 2)
    qi_t = qi_t.reshape(b, nq, IDX_DIM, N_IDX_HEADS * tq).astype(BF16)
    wi = small[:, IDX_DIM:IDX_DIM + N_IDX_HEADS]
    w_t = wi.reshape(b, nq, tq, N_IDX_HEADS).transpose(0, 1, 3, 2).reshape(b, nq, 1, N_IDX_HEADS * tq)
    ki = small[:, :IDX_DIM].reshape(b, t, IDX_DIM).astype(BF16)
    qt = q.astype(BF16).reshape(b, nq, tq, N_HEADS // 2, 2, HEAD_DIM).transpose(0, 1, 3, 4, 5, 2)
    eye = jnp.eye(2, dtype=BF16)
    qbd = qt[:, :, :, :, :, None, :] * eye[None, None, None, :, None, :, None]
    qbd = qbd.reshape(b, nq, N_HEADS // 2, 2 * HEAD_DIM, 2 * tq)
    kb = k.astype(BF16).reshape(b, t, ATTN_WIDTH)
    vt = v.astype(BF16).reshape(b, t // tk, tk, ATTN_WIDTH).transpose(0, 1, 3, 2)
    return qi_t, w_t, ki, qbd, kb, vt


def _sample_group_attention(qs, ks, vs, qis, smalls, page_table, cache_k_l, cache_v_l, cache_ki_l, bd, tn):
    past = page_table.shape[1] * PAGE_SIZE
    n_rows = N_HEADS * tn
    row_pad = ((0, 0), (0, LANES - n_rows), (0, 0))
    key_pad = ((0, 0), (0, 0), (0, PAGE_SIZE - tn))
    head_rows = lambda x: x.reshape(bd, tn, N_HEADS, -1).transpose(0, 2, 1, 3).reshape(bd, n_rows, -1)
    qi_rows = jnp.pad(head_rows(qis), row_pad).astype(BF16)
    w_col = jnp.pad(head_rows(smalls[:, IDX_DIM:IDX_DIM + N_IDX_HEADS]), row_pad)
    ki_new_t = jnp.pad(smalls[:, :IDX_DIM].reshape(bd, tn, IDX_DIM).transpose(0, 2, 1), key_pad)
    n_sel = min(MAX_TOPK, (past + tn) // 4)
    keys, keys_new, thr, need = _sample_score(page_table, qi_rows, w_col, ki_new_t,
                                              cache_ki_l.transpose(0, 1, 3, 2), n_sel, tn)
    q_bd = jnp.einsum('bqhd,hg->bhqgd', qs.reshape(bd, tn, N_HEADS, HEAD_DIM), jnp.eye(N_HEADS, dtype=F32))
    q_bd = jnp.pad(q_bd.reshape(bd, n_rows, ATTN_WIDTH), row_pad).astype(BF16)
    k_new_t = jnp.pad(ks.reshape(bd, tn, ATTN_WIDTH).transpose(0, 2, 1), key_pad)
    v_new_t = jnp.pad(vs.reshape(bd, tn, ATTN_WIDTH).transpose(0, 2, 1), key_pad)
    col = jnp.arange(ATTN_WIDTH)
    diag = (jnp.arange(LANES)[:, None] // tn == col[None, :] // HEAD_DIM).astype(F32)
    fold = (col[:, None] % HEAD_DIM == jnp.arange(HEAD_DIM)[None, :]).astype(BF16)
    earlier = (jnp.arange(PAGE_SIZE)[:, None] < jnp.arange(PAGE_SIZE)[None, :]).astype(BF16)
    tied = (jnp.max(need[:, :, LANES // 2:], axis=(1, 2)) > 0.5).astype(I32)
    out = _sample_attend(page_table, tied, q_bd, keys, keys_new, thr, need, k_new_t, v_new_t, diag, fold,
                         earlier, cache_k_l.transpose(0, 1, 3, 4, 2), cache_v_l.transpose(0, 1, 3, 4, 2))
    return out[:, :n_rows].reshape(bd, N_HEADS, tn, HEAD_DIM).transpose(0, 2, 1, 3).reshape(bd * tn, ATTN_WIDTH)


def _sample_gmlp_weights(w_sp_l, bias_p, tn):
    per_chunk = CHUNK // tn
    tril = jnp.tril(jnp.ones((tn, tn), dtype=bool))
    wsp_s = jnp.einsum('ij,gts->gitjs', jnp.eye(per_chunk, dtype=F32),
                       jnp.where(tril[None], w_sp_l[:, :tn, :tn], 0))
    wsp_s = wsp_s.reshape(N_GROUPS, CHUNK, CHUNK).astype(BF16)
    return wsp_s, jnp.tile(bias_p[:tn], (per_chunk, 1))


def kernel(x_prompt, x_sample, cache_k, cache_v, cache_kidx, page_table, norm_mix_g, w_in,
           gv_norm_g, w_sp, b_sp, w_out, norm_ffn_g, w_pq, peer_sub_keys, expert_u, expert_v,
           norm_final_g):
    b, t, d = x_prompt.shape
    bd, tn, _ = x_sample.shape
    depth = w_in.shape[0]
    assert depth == 1 and tn == 8 and t % 256 == 0 and (bd * tn) % CHUNK == 0
    n_pages = page_table.shape[1]
    past = n_pages * PAGE_SIZE
    l = 0

    wm, ws = _split_w_in(w_in[l])
    g_mix = norm_mix_g[l].reshape(1, d)
    gvg = gv_norm_g[l].reshape(1, GMLP_WIDTH)
    lane_group = jnp.arange(GMLP_WIDTH) // HEAD_DIM
    gmean = (lane_group[:, None] == lane_group[None, :]).astype(BF16) * (1.0 / HEAD_DIM)
    invf = ROPE_THETA ** (-jnp.arange(ROPE_HALF, dtype=F32) / ROPE_HALF)
    invf = jnp.tile(invf, LANES // ROPE_HALF).reshape(1, LANES)
    tril = jnp.tril(jnp.ones((CHUNK, CHUNK), dtype=bool))
    wsp_p = jnp.where(tril[None], w_sp[l], 0).astype(BF16)
    bias_p = jnp.repeat(b_sp[l].T, HEAD_DIM, axis=1)
    wsp_s, bias_s = _sample_gmlp_weights(w_sp[l], bias_p, tn)
    wo =w_out[l].astype(BF16)
    g_ffn = norm_ffn_g[l].reshape(1, d)
    wpqt = w_pq[l].T.astype(BF16)
    sk = peer_sub_keys[l].reshape(2 * N_PEER_HEADS, N_KEYS, PEER_HALF).astype(BF16)
    eu = expert_u[l].astype(BF16)
    evt = expert_v[l].T.astype(BF16)
    gf = norm_final_g.reshape(1, d)

    n_p = b * t
    pos_p = jnp.tile(jnp.arange(t, dtype=F32), b).reshape(n_p, 1)
    q, k, v, qi, u, gv, small = _project(x_prompt.reshape(n_p, d), pos_p, invf, g_mix, wm, ws,
                                         gvg, gmean, 256)
    tq, tk = 256, 256
    ops = _prompt_dsa_operands(q, k, v, qi, small, b, t, tq, tk)
    a_t = _dsa_prompt(*ops, tq, tk, min(MAX_TOPK, t // 4))
    a_p = a_t.transpose(0, 2, 1).reshape(n_p, ATTN_WIDTH).astype(BF16)
    x1_p, hp_p = _mix(a_p, u, gv, x_prompt.reshape(n_p, d), wsp_p, bias_p, wo, g_ffn)

    n_s = bd * tn
    pos_s = jnp.tile(past + jnp.arange(tn, dtype=F32), bd).reshape(n_s, 1)
    qs, ks, vs, qis, us, gvs, smalls = _project(x_sample.reshape(n_s, d), pos_s, invf, g_mix, wm,
                                                ws, gvg, gmean, n_s)
    a_s = _sample_group_attention(qs, ks, vs, qis, smalls, page_table, cache_k, cache_v,
                                  cache_kidx, bd, tn).astype(BF16)
    x1_s, hp_s = _mix(a_s, us, gvs, x_sample.reshape(n_s, d), wsp_s, bias_s, wo, g_ffn)

    x1 = jnp.concatenate([x1_p, x1_s], axis=0)
    hp = jnp.concatenate([hp_p, hp_s], axis=0)
    n_all = n_p + n_s
    tm = 640 if n_all % 640 == 0 else 128
    y = _peer(hp, x1, wpqt, sk, eu, evt, gf, tm, 1024)

    heads = (N_HEADS, HEAD_DIM)
    return (y[:n_p].reshape(b, t, d),
            y[n_p:].reshape(bd, tn, d),
            k.reshape(1, b, t, *heads),
            v.reshape(1, b, t, *heads),
            small[:, :IDX_DIM].reshape(1, b, t, IDX_DIM),
            ks.reshape(1, bd, tn, *heads),
            vs.reshape(1, bd, tn, *heads),
            smalls[:, :IDX_DIM].reshape(1, bd, tn, IDX_DIM),
            gvs.reshape(1, bd, tn, N_GROUPS, HEAD_DIM))
```

```python
import functools

import jax
import jax.numpy as jnp
from jax import lax
from jax.experimental import pallas as pl
from jax.experimental.pallas import tpu as pltpu

F32 = jnp.float32
BF16 = jnp.bfloat16
I32 = jnp.int32

EPS = 1e-6
ROPE_THETA = 10000.0
HEAD_DIM = 64
ROPE_HALF = HEAD_DIM // 2
N_HEADS = 8
ATTN_WIDTH = N_HEADS * HEAD_DIM
N_IDX_HEADS = 8
IDX_DIM = 64
MAX_TOPK = 256
N_GROUPS = 8
GMLP_WIDTH = N_GROUPS * HEAD_DIM
CHUNK = 128
PAGE_SIZE = 128
N_PEER_HEADS = 8
N_KEYS = 128
PEER_HALF = 64
PEER_TOPK = 16
LANES = 128

INT_MIN = -(2 ** 31)
NEG_BIG = -1e30
IDX_SCALE = (IDX_DIM ** -0.5) * (N_IDX_HEADS ** -0.5)
ATTN_SCALE = HEAD_DIM ** -0.5
LOG2E = 1.4426950408889634

VMEM_LIMIT_BYTES = 56 * 1024 * 1024


def _params(*sem):
    return pltpu.CompilerParams(dimension_semantics=sem, vmem_limit_bytes=VMEM_LIMIT_BYTES)


def _gelu(x):
    cdf = 0.5 * (1.0 + jnp.tanh(0.7978845608028654 * (x + 0.044715 * (x * x * x))))
    return x * cdf


def _rms(x, g):
    return x * lax.rsqrt(jnp.mean(x * x, axis=-1, keepdims=True) + EPS) * g


def _split3(x):
    a = x.astype(BF16)
    r = x - a.astype(F32)
    b = r.astype(BF16)
    c = (r - b.astype(F32)).astype(BF16)
    return a, b, c


def _dot_f32_by_bf16(x, m):
    a, b, c = _split3(x)
    d = lambda p: jnp.dot(p, m, preferred_element_type=F32)
    return d(a) + d(b) + d(c)


def _code_to_f32(code):
    bits = jnp.where(code < 0, code ^ 0x7FFFFFFF, code)
    return lax.bitcast_convert_type(bits, F32)


LOWEST_FINITE_CODE = -(2 ** 31) + 0x00800000


COUNT_ROWS = 64


def _count_partial(mask):
    rows, n = mask.shape
    ones = jnp.where(mask, 1.0, 0.0)
    return jnp.sum(ones.reshape(rows // COUNT_ROWS, COUNT_ROWS, n), axis=0)


def _kth_largest_key(count_ge, k, shape):
    c0 = count_ge(jnp.zeros(shape, I32))
    t0 = jnp.where(c0 >= k, 0, INT_MIN).astype(I32)

    def body(i, t):
        cand = t + lax.shift_left(jnp.int32(1), 30 - i)
        return jnp.where(count_ge(cand) >= k, cand, t)

    return lax.fori_loop(0, 31, body, t0)


SMALLEST_NORMAL_CODE = 0x00800000


def _second_cut(count_above_ge, k, shape):
    smallest_normal = jnp.full(shape, SMALLEST_NORMAL_CODE, I32)
    cut2 = lax.cond(jnp.max(count_above_ge(smallest_normal)) > 0.5,
                    lambda: _kth_largest_key(count_above_ge, k, shape),
                    lambda: jnp.zeros(shape, I32))
    tiny = jnp.logical_and(cut2 >= 0, cut2 < SMALLEST_NORMAL_CODE)
    thr2 = jnp.where(tiny, 0.0, _code_to_f32(cut2))
    need2 = k - count_above_ge(jnp.where(tiny, SMALLEST_NORMAL_CODE, cut2 + 1))
    return thr2, need2


def _project_kernel(x_ref, pos_ref, invf_ref, g_ref, wm_ref, ws_ref, gvg_ref, gmean_ref,
                    q_ref, k_ref, v_ref, qi_ref, u_ref, gv_ref, small_ref):
    h = _rms(x_ref[...], g_ref[...]).astype(BF16)
    ang = pos_ref[...] * invf_ref[...]
    cos = jnp.cos(ang)
    sin = jnp.sin(ang)
    lane = lax.broadcasted_iota(I32, ang.shape, 1)
    first = (lane % HEAD_DIM) < ROPE_HALF
    sin_signed = jnp.where(first, -sin, sin)

    def rope(xc):
        partner = jnp.where(first, pltpu.roll(xc, LANES - ROPE_HALF, 1), pltpu.roll(xc, ROPE_HALF, 1))
        return xc * cos + partner * sin_signed

    def proj(c0):
        return jnp.dot(h, wm_ref[:, c0:c0 + ATTN_WIDTH], preferred_element_type=F32)

    for ref, c0 in ((q_ref, 0), (k_ref, ATTN_WIDTH), (qi_ref, 3 * ATTN_WIDTH)):
        p = proj(c0)
        for c in range(ATTN_WIDTH // LANES):
            ref[:, c * LANES:(c + 1) * LANES] = rope(p[:, c * LANES:(c + 1) * LANES])
    v_ref[...] = proj(2 * ATTN_WIDTH)
    u_ref[...] = _gelu(proj(4 * ATTN_WIDTH))
    gl = _gelu(proj(5 * ATTN_WIDTH))
    ms = _dot_f32_by_bf16(gl * gl, gmean_ref[...])
    gv_ref[...] = gl * lax.rsqrt(ms + EPS) * gvg_ref[...]
    sm = jnp.dot(h, ws_ref[...], preferred_element_type=F32)
    small_ref[...] = jnp.where(lane < IDX_DIM, rope(sm), sm)


def _project(x2d, pos, invf, g, wm, ws, gvg, gmean, tm):
    n, d = x2d.shape
    row = lambda i: (i, 0)
    fixed = lambda i: (0, 0)
    wide = jax.ShapeDtypeStruct((n, ATTN_WIDTH), F32)
    return pl.pallas_call(
        _project_kernel,
        grid=(n // tm,),
        in_specs=[
            pl.BlockSpec((tm, d), row),
            pl.BlockSpec((tm, 1), row),
            pl.BlockSpec((1, LANES), fixed),
            pl.BlockSpec((1, d), fixed),
            pl.BlockSpec(wm.shape, fixed),
            pl.BlockSpec(ws.shape, fixed),
            pl.BlockSpec((1, GMLP_WIDTH), fixed),
            pl.BlockSpec(gmean.shape, fixed),
        ],
        out_specs=[pl.BlockSpec((tm, ATTN_WIDTH), row)] * 6 + [pl.BlockSpec((tm, LANES), row)],
        out_shape=[wide] * 6 + [jax.ShapeDtypeStruct((n, LANES), F32)],
        compiler_params=_params("parallel"),
        name="project",
    )(x2d, pos, invf, g, wm, ws, gvg, gmean)


def _dsa_prompt_kernel(qi_ref, w_ref, ki_ref, qbd_ref, k_ref, vt_ref, out_ref, keys_ref, acc_ref, lg_ref,
                       *, tq, tk, n_sel):
    j = pl.program_id(1)
    n_tiles = pl.num_programs(1)
    sub = tq // tk
    nkb = (j + 1) * sub
    q0 = j * tq
    qi = qi_ref[0, 0]
    w = w_ref[0, 0] * IDX_SCALE
    qpos = q0 + lax.broadcasted_iota(I32, (tk, tq), 1)
    krow = lax.broadcasted_iota(I32, (tk, tq), 0)

    def score_body(kq, carry):
        for u in range(sub):
            r0 = pl.multiple_of(kq * tq, tq) + u * tk
            d = jnp.dot(ki_ref[0, pl.ds(r0, tk), :], qi, preferred_element_type=F32)
            s = jnp.zeros((tk, tq), F32)
            for h in range(N_IDX_HEADS):
                s = s + jnp.maximum(d[:, h * tq:(h + 1) * tq], 0.0) * w[:, h * tq:(h + 1) * tq]
            keys_ref[pl.ds(r0, tk), :] = jnp.where(krow + r0 <= qpos, s, -jnp.inf)
        return carry

    lax.fori_loop(0, j + 1, score_body, 0)

    @pl.when(jnp.logical_and((j + 1) % 2 == 1, j + 1 < n_tiles))
    def _blank():
        keys_ref[pl.ds(pl.multiple_of((j + 1) * tq, tq), tq), :] = jnp.full((tq, tq), -jnp.inf, F32)

    def count_ge(code):
        cand = _code_to_f32(code)

        def body(kb, acc):
            blk = keys_ref[pl.ds(pl.multiple_of(kb * 2 * tq, 2 * tq), 2 * tq), :]
            return acc + _count_partial(blk >= cand)
        acc = lax.fori_loop(0, (j + 2) // 2, body, jnp.zeros((COUNT_ROWS, tq), F32))
        return jnp.sum(acc, axis=0, keepdims=True)

    cut = jnp.maximum(_kth_largest_key(count_ge, n_sel, (1, tq)), LOWEST_FINITE_CODE)
    thr = _code_to_f32(cut)

    @pl.when(jnp.max(count_ge(cut)) > n_sel + 0.5)
    def _break_ties():
        blk_rows = 2 * tq
        n_blocks = (j + 2) // 2
        thr_next = _code_to_f32(cut + 1)
        need = n_sel - count_ge(cut + 1)

        def load(kb):
            rows = pl.ds(pl.multiple_of(kb * blk_rows, blk_rows), blk_rows)
            blk = keys_ref[rows, :]
            above = jnp.where(blk >= thr, jnp.where(blk < thr_next, blk - thr, -jnp.inf), -jnp.inf)
            return rows, blk, above

        def count_above_ge(code):
            cand = _code_to_f32(code)

            def body(kb, acc):
                return acc + _count_partial(load(kb)[2] >= cand)
            acc = lax.fori_loop(0, n_blocks, body, jnp.zeros((COUNT_ROWS, tq), F32))
            return jnp.sum(acc, axis=0, keepdims=True)

        thr2, need2 = _second_cut(count_above_ge, need, (1, tq))
        earlier = (lax.broadcasted_iota(I32, (blk_rows, blk_rows), 1)
                   < lax.broadcasted_iota(I32, (blk_rows, blk_rows), 0))
        earlier = jnp.where(earlier, 1.0, 0.0).astype(BF16)

        def body(kb, seen):
            rows, blk, above = load(kb)
            eq = jnp.where(above == thr2, 1.0, 0.0)
            before = jnp.dot(earlier, eq.astype(BF16), preferred_element_type=F32) + seen
            late = eq * before >= jnp.maximum(need2, 0.5)
            keys_ref[rows, :] = jnp.where(above < thr2, jnp.where(above >= 0.0, -jnp.inf, blk),
                                          jnp.where(late, -jnp.inf, blk))
            return seen + jnp.sum(eq, axis=0, keepdims=True)

        lax.fori_loop(0, n_blocks, body, jnp.zeros((1, tq), F32))

    n_pairs = N_HEADS // 2
    acc_ref[...] = jnp.zeros(acc_ref.shape, F32)

    def logits(kb):
        r0 = pl.multiple_of(kb * tk, tk)
        return [jnp.dot(k_ref[0, pl.ds(r0, tk), p * LANES:(p + 1) * LANES], qbd_ref[0, 0, p],
                        preferred_element_type=F32) for p in range(n_pairs)]

    def stash(slot, lgs):
        for p in range(n_pairs):
            lg_ref[slot, :, p * 2 * tq:(p + 1) * 2 * tq] = lgs[p]

    stash(0, logits(0))

    def att_body(kb, carry):
        ms, ls = carry
        r0 = pl.multiple_of(kb * tk, tk)
        slot = kb % 2
        nxt = logits(jnp.minimum(kb + 1, nkb - 1))
        sel = keys_ref[pl.ds(r0, tk), :] >= thr
        new_ms, new_ls = [], []
        for p in range(n_pairs):
            rows = slice(p * LANES, (p + 1) * LANES)
            lg = lg_ref[slot, :, p * 2 * tq:(p + 1) * 2 * tq] * (ATTN_SCALE * LOG2E)
            lm = jnp.concatenate([jnp.where(sel, lg[:, :tq], NEG_BIG),
                                  jnp.where(sel, lg[:, tq:], NEG_BIG)], axis=1)
            m_new = jnp.maximum(ms[p], jnp.max(lm, axis=0, keepdims=True))
            pe = jnp.exp2(lm - m_new)
            alpha = jnp.exp2(ms[p] - m_new)
            new_ls.append(alpha * ls[p] + jnp.sum(pe, axis=0, keepdims=True))
            new_ms.append(m_new)
            pb = pe.astype(BF16)
            vtb = vt_ref[0, kb, rows, :]
            pv0 = jnp.dot(vtb[:HEAD_DIM], pb[:, :tq], preferred_element_type=F32)
            pv1 = jnp.dot(vtb[HEAD_DIM:], pb[:, tq:], preferred_element_type=F32)
            acc = acc_ref[rows, :]
            acc_ref[rows, :] = jnp.concatenate([alpha[:, :tq] * acc[:HEAD_DIM] + pv0,
                                                alpha[:, tq:] * acc[HEAD_DIM:] + pv1], axis=0)
        stash(1 - slot, nxt)
        return tuple(new_ms), tuple(new_ls)

    init = (tuple(jnp.full((1, 2 * tq), NEG_BIG, F32) for _ in range(n_pairs)),
            tuple(jnp.zeros((1, 2 * tq), F32) for _ in range(n_pairs)))
    _, ls = lax.fori_loop(0, nkb, att_body, init)
    for p in range(n_pairs):
        rows = slice(p * LANES, (p + 1) * LANES)
        acc = acc_ref[rows, :]
        out_ref[0, rows, :] = jnp.concatenate(
            [acc[:HEAD_DIM] / ls[p][:, :tq], acc[HEAD_DIM:] / ls[p][:, tq:]], axis=0)


def _dsa_prompt(qi_t, w_t, ki, qbd, k, vt, tq, tk, n_sel):
    b, nq = qi_t.shape[:2]
    t = k.shape[1]
    tile = lambda bb, j: (bb, j, 0, 0)
    whole3 = lambda bb, j: (bb, 0, 0)
    return pl.pallas_call(
        functools.partial(_dsa_prompt_kernel, tq=tq, tk=tk, n_sel=n_sel),
        grid=(b, nq),
        in_specs=[
            pl.BlockSpec((1, 1, IDX_DIM, N_IDX_HEADS * tq), tile),
            pl.BlockSpec((1, 1, 1, N_IDX_HEADS * tq), tile),
            pl.BlockSpec((1, t, IDX_DIM), whole3, pipeline_mode=pl.Buffered(1)),
            pl.BlockSpec((1, 1, N_HEADS // 2, LANES, 2 * tq), lambda bb, j: (bb, j, 0, 0, 0)),
            pl.BlockSpec((1, t, ATTN_WIDTH), whole3, pipeline_mode=pl.Buffered(1)),
            pl.BlockSpec((1, t // tk, ATTN_WIDTH, tk), lambda bb, j: (bb, 0, 0, 0),
                         pipeline_mode=pl.Buffered(1)),
        ],
        out_specs=pl.BlockSpec((1, ATTN_WIDTH, tq), lambda bb, j: (bb, 0, j)),
        out_shape=jax.ShapeDtypeStruct((b, ATTN_WIDTH, t), F32),
        scratch_shapes=[pltpu.VMEM((t, tq), F32), pltpu.VMEM((ATTN_WIDTH, tq), F32),
                        pltpu.VMEM((2, tk, N_HEADS * tq), F32)],
        compiler_params=_params("parallel", "parallel"),
        name="dsa_prompt",
    )(qi_t, w_t, ki, qbd, k, vt)


PAGES_PER_STEP = 8
STEP_KEYS = PAGES_PER_STEP * PAGE_SIZE


def _sample_score_kernel(pt_ref, qi_ref, w_ref, kinew_ref, *rest, n_sel, n_new):
    pages = rest[:PAGES_PER_STEP]
    keys_ref, keysnew_ref, thr_ref, need_ref, scr = rest[PAGES_PER_STEP:]
    s = pl.program_id(1)
    n_steps = pl.num_programs(1)
    qi = qi_ref[...]
    w = w_ref[...] * IDX_SCALE

    def score(ki_t):
        d = jnp.dot(qi, ki_t.astype(BF16), preferred_element_type=F32)
        sc = jnp.maximum(d, 0.0) * w
        tot = jnp.zeros((n_new, ki_t.shape[1]), F32)
        for h in range(N_IDX_HEADS):
            tot = tot + sc[h * n_new:(h + 1) * n_new]
        return tot

    key = score(jnp.concatenate([page[...] for page in pages], axis=1))
    keys_ref[...] = key
    scr[s] = key

    @pl.when(s == n_steps - 1)
    def _finish():
        key = score(kinew_ref[...])
        qrow = lax.broadcasted_iota(I32, key.shape, 0)
        kcol = lax.broadcasted_iota(I32, key.shape, 1)
        key = jnp.where(kcol <= qrow, key, -jnp.inf)
        keysnew_ref[...] = key
        scr[n_steps] = jnp.concatenate(
            [key, jnp.full((n_new, STEP_KEYS - PAGE_SIZE), -jnp.inf, F32)], axis=1)

        def count_ge(code):
            cand = _code_to_f32(code)
            acc = jnp.zeros((n_new, STEP_KEYS), F32)
            for slab in range(scr.shape[0]):
                acc = acc + jnp.where(scr[slab] >= cand, 1.0, 0.0)
            return jnp.sum(acc, axis=1, keepdims=True)

        cut = jnp.maximum(_kth_largest_key(count_ge, n_sel, (n_new, 1)), LOWEST_FINITE_CODE)
        thr = _code_to_f32(cut)
        thr_next = _code_to_f32(cut + 1)
        need = n_sel - count_ge(cut + 1)

        def count_above_ge(code):
            cand = _code_to_f32(code)
            acc = jnp.zeros((n_new, STEP_KEYS), F32)
            for slab in range(scr.shape[0]):
                sc = scr[slab]
                above = jnp.where(sc >= thr, jnp.where(sc < thr_next, sc - thr, -jnp.inf), -jnp.inf)
                acc = acc + jnp.where(above >= cand, 1.0, 0.0)
            return jnp.sum(acc, axis=1, keepdims=True)

        thr2, need2 = _second_cut(count_above_ge, need, (n_new, 1))
        surplus = count_ge(cut) - n_sel
        lane = lax.broadcasted_iota(I32, thr_ref.shape, 1)
        wide = lambda x: jnp.broadcast_to(x, thr_ref.shape)
        thr_ref[...] = jnp.where(lane < LANES // 2, wide(thr), wide(thr_next))
        need_ref[...] = jnp.where(lane < LANES // 4, wide(thr2),
                                  jnp.where(lane < LANES // 2, wide(need2), wide(surplus)))


def _page_spec(tail, k):
    zeros = (0,) * len(tail)
    return pl.BlockSpec((None, None) + tail, lambda b, s, pt, *_: (0, pt[b, s * PAGES_PER_STEP + k]) + zeros)


def _sample_score(page_table, qi_rows, w_col, ki_new, cache_ki, n_sel, n_new):
    nb, n_pages = page_table.shape
    n_steps = n_pages // PAGES_PER_STEP
    per_b = lambda b, s, pt: (b, 0, 0)
    grid_spec = pltpu.PrefetchScalarGridSpec(
        num_scalar_prefetch=1,
        grid=(nb, n_steps),
        in_specs=[
            pl.BlockSpec((None, LANES, IDX_DIM), per_b),
            pl.BlockSpec((None, LANES, 1), per_b),
            pl.BlockSpec((None, IDX_DIM, PAGE_SIZE), per_b),
        ] + [_page_spec((IDX_DIM, PAGE_SIZE), k) for k in range(PAGES_PER_STEP)],
        out_specs=[
            pl.BlockSpec((None, n_new, STEP_KEYS), lambda b, s, pt: (b, 0, s)),
            pl.BlockSpec((None, n_new, PAGE_SIZE), per_b),
            pl.BlockSpec((None, n_new, LANES), per_b),
            pl.BlockSpec((None, n_new, LANES), per_b),
        ],
        scratch_shapes=[pltpu.VMEM((n_steps + 1, n_new, STEP_KEYS), F32)],
    )
    return pl.pallas_call(
        functools.partial(_sample_score_kernel, n_sel=n_sel, n_new=n_new),
        grid_spec=grid_spec,
        out_shape=[
            jax.ShapeDtypeStruct((nb, n_new, n_pages * PAGE_SIZE), F32),
            jax.ShapeDtypeStruct((nb, n_new, PAGE_SIZE), F32),
            jax.ShapeDtypeStruct((nb, n_new, LANES), F32),
            jax.ShapeDtypeStruct((nb, n_new, LANES), F32),
        ],
        compiler_params=_params("parallel", "arbitrary"),
        name="sample_score",
    )(page_table, qi_rows, w_col, ki_new, *([cache_ki] * PAGES_PER_STEP))


def _sample_attend_kernel(pt_ref, tied_ref, q_ref, keys_ref, keysnew_ref, thr_ref, need_ref, knew_ref, vnew_ref,
                          diag_ref, fold_ref, earlier_ref, *rest):
    kpages = rest[:PAGES_PER_STEP]
    vpages = rest[PAGES_PER_STEP:2 * PAGES_PER_STEP]
    out_ref, m_ref, l_ref, acc_ref, seen_ref = rest[2 * PAGES_PER_STEP:]
    s = pl.program_id(1)
    n_steps = pl.num_programs(1)
    n_new = keys_ref.shape[0]
    nt = (((1,), (1,)), ((), ()))

    @pl.when(s == 0)
    def _init():
        m_ref[...] = jnp.full(m_ref.shape, NEG_BIG, F32)
        l_ref[...] = jnp.zeros(l_ref.shape, F32)
        acc_ref[...] = jnp.zeros(acc_ref.shape, F32)
        seen_ref[...] = jnp.zeros(seen_ref.shape, F32)

    q = q_ref[...]
    thr = thr_ref[:, 0:1]
    thr_next = thr_ref[:, LANES // 2:LANES // 2 + 1]
    thr2 = need_ref[:, 0:1]
    need2 = need_ref[:, LANES // 4:LANES // 4 + 1]

    tied = tied_ref[pl.program_id(0)] != 0

    def selected_in_order(keyblk):
        above = jnp.where(keyblk >= thr, jnp.where(keyblk < thr_next, keyblk - thr, -jnp.inf), -jnp.inf)
        eq = jnp.where(above == thr2, 1.0, 0.0)
        seen = seen_ref[...]
        before = jnp.dot(eq.astype(BF16), earlier_ref[...], preferred_element_type=F32) + seen
        seen_ref[...] = seen + jnp.sum(eq, axis=1, keepdims=True)
        return jnp.where(keyblk >= thr_next, 1.0,
                         jnp.where(above > thr2, 1.0, jnp.where(before < need2, eq, 0.0)))

    def selected(keys):
        def in_order(kk):
            return jnp.concatenate([selected_in_order(kk[:, i:i + PAGE_SIZE])
                                    for i in range(0, kk.shape[1], PAGE_SIZE)], axis=1)
        return lax.cond(tied, in_order, lambda kk: jnp.where(kk >= thr, 1.0, 0.0), keys)

    def accumulate(k_t, v_t, chosen):
        lg = jnp.dot(q, k_t.astype(BF16), preferred_element_type=F32) * (ATTN_SCALE * LOG2E)
        sel = jnp.concatenate([chosen] * (LANES // n_new), axis=0) > 0.5
        lm = jnp.where(sel, lg, NEG_BIG)
        m = m_ref[...]
        m_new = jnp.maximum(m, jnp.max(lm, axis=1, keepdims=True))
        pe = jnp.exp2(lm - m_new)
        alpha = jnp.exp2(m - m_new)
        l_ref[...] = alpha * l_ref[...] + jnp.sum(pe, axis=1, keepdims=True)
        pv = lax.dot_general(pe.astype(BF16), v_t.astype(BF16), nt, preferred_element_type=F32)
        acc_ref[...] = alpha * acc_ref[...] + pv
        m_ref[...] = m_new

    flat = (ATTN_WIDTH, PAGE_SIZE)
    accumulate(jnp.concatenate([p[...].reshape(flat) for p in kpages], axis=1),
               jnp.concatenate([p[...].reshape(flat) for p in vpages], axis=1),
               selected(keys_ref[...]))

    @pl.when(s == n_steps - 1)
    def _finish():
        accumulate(knew_ref[...], vnew_ref[...], selected(keysnew_ref[...]))
        own_head = acc_ref[...] / l_ref[...] * diag_ref[...]
        out_ref[...] = _dot_f32_by_bf16(own_head, fold_ref[...])


def _sample_attend(page_table, tied, q_rows, keys, keys_new, thr, need, k_new, v_new, diag, fold, earlier,
                   cache_k_t, cache_v_t):
    nb, n_pages = page_table.shape
    n_steps = n_pages // PAGES_PER_STEP
    n_new = keys.shape[1]
    per_b = lambda b, s, pt, tied: (b, 0, 0)
    fixed = lambda b, s, pt, tied: (0, 0)
    page = (N_HEADS, HEAD_DIM, PAGE_SIZE)
    grid_spec = pltpu.PrefetchScalarGridSpec(
        num_scalar_prefetch=2,
        grid=(nb, n_steps),
        in_specs=[
            pl.BlockSpec((None, LANES, ATTN_WIDTH), per_b),
            pl.BlockSpec((None, n_new, STEP_KEYS), lambda b, s, pt, tied: (b, 0, s)),
            pl.BlockSpec((None, n_new, PAGE_SIZE), per_b),
            pl.BlockSpec((None, n_new, LANES), per_b),
            pl.BlockSpec((None, n_new, LANES), per_b),
            pl.BlockSpec((None, ATTN_WIDTH, PAGE_SIZE), per_b),
            pl.BlockSpec((None, ATTN_WIDTH, PAGE_SIZE), per_b),
            pl.BlockSpec(diag.shape, fixed),
            pl.BlockSpec(fold.shape, fixed),
            pl.BlockSpec(earlier.shape, fixed),
        ] + [_page_spec(page, k) for k in range(PAGES_PER_STEP)] * 2,
        out_specs=pl.BlockSpec((None, LANES, HEAD_DIM), per_b),
        scratch_shapes=[pltpu.VMEM((LANES, 1), F32), pltpu.VMEM((LANES, 1), F32),
                        pltpu.VMEM((LANES, ATTN_WIDTH), F32), pltpu.VMEM((n_new, 1), F32)],
    )
    return pl.pallas_call(
        _sample_attend_kernel,
        grid_spec=grid_spec,
        out_shape=jax.ShapeDtypeStruct((nb, LANES, HEAD_DIM), F32),
        compiler_params=_params("parallel", "arbitrary"),
        name="sample_attend",
    )(page_table, tied, q_rows, keys, keys_new, thr, need, k_new, v_new, diag, fold, earlier,
      *([cache_k_t] * PAGES_PER_STEP), *([cache_v_t] * PAGES_PER_STEP))


def _mix_kernel(a_ref, u_ref, gv_ref, x_ref, wsp_ref, bias_ref, wo_ref, g_ref, x1_ref, hp_ref):
    gvb = gv_ref[...].astype(BF16)
    group = lax.broadcasted_iota(I32, gvb.shape, 1) // HEAD_DIM
    mixed = bias_ref[...]
    for g in range(N_GROUPS):
        r = jnp.dot(wsp_ref[g], gvb, preferred_element_type=F32)
        mixed = mixed + jnp.where(group == g, r, 0.0)
    gm = (u_ref[...] * mixed).astype(BF16)
    mix = (jnp.dot(a_ref[...], wo_ref[:ATTN_WIDTH, :], preferred_element_type=F32)
           + jnp.dot(gm, wo_ref[ATTN_WIDTH:, :], preferred_element_type=F32))
    x1 = x_ref[...] + mix
    x1_ref[...] = x1
    hp_ref[...] = _rms(x1, g_ref[...]).astype(BF16)


def _mix(a, u, gv, x, wsp, bias, wo, g):
    n, d = x.shape
    row = lambda i: (i, 0)
    fixed = lambda i: (0, 0)
    return pl.pallas_call(
        _mix_kernel,
        grid=(n // CHUNK,),
        in_specs=[
            pl.BlockSpec((CHUNK, ATTN_WIDTH), row),
            pl.BlockSpec((CHUNK, GMLP_WIDTH), row),
            pl.BlockSpec((CHUNK, GMLP_WIDTH), row),
            pl.BlockSpec((CHUNK, d), row),
            pl.BlockSpec(wsp.shape, lambda i: (0, 0, 0)),
            pl.BlockSpec(bias.shape, fixed),
            pl.BlockSpec(wo.shape, fixed),
            pl.BlockSpec((1, d), fixed),
        ],
        out_specs=[pl.BlockSpec((CHUNK, d), row), pl.BlockSpec((CHUNK, d), row)],
        out_shape=[jax.ShapeDtypeStruct((n, d), F32), jax.ShapeDtypeStruct((n, d), BF16)],
        compiler_params=_params("parallel"),
        name="mix",
    )(a, u, gv, x, wsp, bias, wo, g)


def _col_reduce(x, op):
    slabs = [x[i:i + 8] for i in range(0, x.shape[0], 8)]
    while len(slabs) > 1:
        nxt = [op(slabs[i], slabs[i + 1]) for i in range(0, len(slabs) - 1, 2)]
        if len(slabs) % 2:
            nxt.append(slabs[-1])
        slabs = nxt
    red = jnp.max if op is jnp.maximum else jnp.min
    return red(slabs[0], axis=0, keepdims=True)


def _extract16(arrays):
    rows, n = arrays[0].shape
    ridx = lax.broadcasted_iota(I32, (rows, n), 0).astype(F32)
    r16 = lax.broadcasted_iota(I32, (PEER_TOPK, n), 0)

    def body(r, carry):
        out = []
        for cur, vals, rank in carry:
            mx = _col_reduce(cur, jnp.maximum)
            hit = ridx == _col_reduce(jnp.where(cur == mx, ridx, float(rows)), jnp.minimum)
            out.append((jnp.where(hit, -jnp.inf, cur), jnp.where(r16 == r, mx, vals),
                        jnp.where(hit, lax.convert_element_type(r, F32), rank)))
        return tuple(out)

    init = tuple((s, jnp.zeros((PEER_TOPK, n), F32), jnp.full((rows, n), float(PEER_TOPK), F32))
                 for s in arrays)
    res = lax.fori_loop(0, PEER_TOPK, body, init)
    return [(vals, rank) for _, vals, rank in res]


def _batcher_pairs(n):
    pairs = []

    def merge(lo, hi, r):
        step = r * 2
        if step < hi - lo:
            merge(lo, hi, step)
            merge(lo + r, hi, step)
            pairs.extend((i, i + r) for i in range(lo + r, hi - r, step))
        else:
            pairs.append((lo, lo + r))

    def sort(lo, hi):
        if hi - lo >= 1:
            mid = lo + (hi - lo) // 2
            sort(lo, mid)
            sort(mid + 1, hi)
            merge(lo, hi, 1)

    sort(0, n - 1)
    return pairs


_SORT16 = _batcher_pairs(PEER_TOPK)
SUBLANES = 8


def _compare_exchange(v, i, j):
    v[i], v[j] = jnp.maximum(v[i], v[j]), jnp.minimum(v[i], v[j])


def _merge_top16(a, b):
    v = [jnp.maximum(a[i], b[PEER_TOPK - 1 - i]) for i in range(PEER_TOPK)]
    d = PEER_TOPK // 2
    while d:
        for i in range(PEER_TOPK):
            if not i & d:
                _compare_exchange(v, i, i + d)
        d //= 2
    return v


def _fold_sublanes(v):
    for sh in (4, 2, 1):
        v = _merge_top16(v, [pltpu.roll(x, sh, 0) for x in v])
    return v


def _top16_values(x):
    v = [x[i * SUBLANES:(i + 1) * SUBLANES] for i in range(x.shape[0] // SUBLANES)]
    for i, j in _SORT16:
        _compare_exchange(v, i, j)
    return _fold_sublanes(v)


def _rows(slab, n_slabs):
    return jnp.concatenate([slab] * n_slabs, axis=0)


def _peer_gates_fast(s1, s2):
    n_slabs = s1.shape[0] // SUBLANES
    a = _top16_values(s1)
    b = _top16_values(s2)
    sub = lax.broadcasted_iota(I32, a[0].shape, 0)

    def spread(vals, g):
        out = vals[g * SUBLANES + SUBLANES - 1]
        for t in range(SUBLANES - 2, -1, -1):
            out = jnp.where(sub == t, vals[g * SUBLANES + t], out)
        return out

    b_lo, b_hi = spread(b, 0), spread(b, 1)
    cands = [x + b_lo for x in a] + [x + b_hi for x in a]
    best = _fold_sublanes(_merge_top16(cands[:PEER_TOPK], cands[PEER_TOPK:]))
    tau = best[PEER_TOPK - 1]
    z = functools.reduce(jnp.add, [jnp.exp(x - best[0]) for x in best])
    in1 = s1 >= _rows(a[PEER_TOPK - 1], n_slabs)
    in2 = s2 >= _rows(b[PEER_TOPK - 1], n_slabs)
    e1 = jnp.where(in1, jnp.exp(s1 - _rows(a[0], n_slabs)), 0.0)
    e2 = jnp.where(in2, jnp.exp(s2 - _rows(b[0], n_slabs)), 0.0) / _rows(z, n_slabs)
    count = lambda m: jnp.sum(jnp.where(m, 1.0, 0.0), axis=0, keepdims=True)
    n_pairs = count(jnp.concatenate(cands, axis=0) >= _rows(tau, 2 * PEER_TOPK))
    tied = jnp.max(jnp.maximum(jnp.maximum(count(in1), count(in2)), n_pairs)) > PEER_TOPK + 0.5
    return e1, e2, tau, tied


def _peer_gates_exact(s1, s2):
    n = s1.shape[1]
    (a, k1), (b, k2) = _extract16([s1, s2])
    cand = jnp.concatenate([a[r:r + 1, :] + b for r in range(PEER_TOPK)], axis=0)
    (best, kc), = _extract16([cand])
    z = jnp.sum(jnp.exp(best - best[0:1, :]), axis=0, keepdims=True)
    e1 = jnp.where(k1 < PEER_TOPK, jnp.exp(s1 - a[0:1, :]), 0.0)
    e2 = jnp.where(k2 < PEER_TOPK, jnp.exp(s2 - b[0:1, :]), 0.0) / z
    taken = jnp.where(kc < PEER_TOPK, 1.0, 0.0)
    r1 = jnp.zeros_like(s1)
    for r in range(PEER_TOPK):
        row_len = jnp.sum(taken[r * PEER_TOPK:(r + 1) * PEER_TOPK], axis=0, keepdims=True)
        r1 = r1 + jnp.where(k1 == r, row_len, 0.0)
    r2 = jnp.where(k2 < PEER_TOPK, -k2, -4.0 * PEER_TOPK)
    return e1, e2, r1, r2, jnp.full((SUBLANES, n), 0.5, F32)


def _peer_kernel(hp_ref, x1_ref, wpqt_ref, sk_ref, u_ref, vt_ref, gf_ref, y_ref,
                 s1_ref, s2_ref, e1_ref, e2_ref, tau_ref, out_ref, *, tm, eb):
    e = pl.program_id(1)
    n_e = pl.num_programs(1)
    hp = hp_ref[...]
    nt = (((1,), (1,)), ((), ()))

    @pl.when(e == 0)
    def _select():
        qt = lax.dot_general(wpqt_ref[...], hp, nt, preferred_element_type=F32).astype(BF16)
        for hh in range(N_PEER_HEADS):
            for c, ref in enumerate((s1_ref, s2_ref)):
                idx = hh * 2 + c
                ref[hh] = jnp.dot(sk_ref[idx], qt[idx * PEER_HALF:(idx + 1) * PEER_HALF, :],
                                  preferred_element_type=F32)

        def head_body(hh, carry):
            for ch in range(tm // LANES):
                sl = slice(ch * LANES, (ch + 1) * LANES)
                s1 = s1_ref[hh, :, sl]
                s2 = s2_ref[hh, :, sl]
                e1, e2, tau, tied = _peer_gates_fast(s1, s2)
                e1_ref[hh, :, sl] = e1
                e2_ref[hh, :, sl] = e2
                tau_ref[hh, :, sl] = tau

                @pl.when(tied)
                def _redo(hh=hh, sl=sl, s1=s1, s2=s2):
                    e1x, e2x, r1, r2, cut = _peer_gates_exact(s1, s2)
                    e1_ref[hh, :, sl] = e1x
                    e2_ref[hh, :, sl] = e2x
                    s1_ref[hh, :, sl] = r1
                    s2_ref[hh, :, sl] = r2
                    tau_ref[hh, :, sl] = cut
            return carry

        lax.fori_loop(0, N_PEER_HEADS, head_body, 0)
        out_ref[...] = jnp.zeros(out_ref.shape, F32)

    act = _gelu(lax.dot_general(u_ref[...], hp, nt, preferred_element_type=F32))
    parts = []
    for ii in range(eb // N_KEYS):
        i = e * (eb // N_KEYS) + ii
        wt = jnp.zeros((N_KEYS, tm), F32)
        for hh in range(N_PEER_HEADS):
            cs = s1_ref[hh, pl.ds(i, 1), :] + s2_ref[hh]
            wt = wt + jnp.where(cs >= tau_ref[hh, 0:1, :],
                                e1_ref[hh, pl.ds(i, 1), :] * e2_ref[hh], 0.0)
        parts.append((wt * act[ii * N_KEYS:(ii + 1) * N_KEYS, :]).astype(BF16))
    gt = jnp.concatenate(parts, axis=0)
    out_ref[...] += jnp.dot(vt_ref[...], gt, preferred_element_type=F32)

    @pl.when(e == n_e - 1)
    def _finish():
        y_ref[...] = _rms(x1_ref[...] + out_ref[...].T, gf_ref[...])


def _peer(hp, x1, wpqt, sk, eu, evt, gf, tm, eb):
    n, d = x1.shape
    n_exp = eu.shape[0]
    row = lambda t, e: (t, 0)
    fixed = lambda t, e: (0, 0)
    head_scr = pltpu.VMEM((N_PEER_HEADS, N_KEYS, tm), F32)
    return pl.pallas_call(
        functools.partial(_peer_kernel, tm=tm, eb=eb),
        grid=(n // tm, n_exp // eb),
        in_specs=[
            pl.BlockSpec((tm, d), row),
            pl.BlockSpec((tm, d), row),
            pl.BlockSpec(wpqt.shape, fixed),
            pl.BlockSpec(sk.shape, lambda t, e: (0, 0, 0)),
            pl.BlockSpec((eb, d), lambda t, e: (e, 0)),
            pl.BlockSpec((d, eb), lambda t, e: (0, e)),
            pl.BlockSpec((1, d), fixed),
        ],
        out_specs=pl.BlockSpec((tm, d), row),
        out_shape=jax.ShapeDtypeStruct((n, d), F32),
        scratch_shapes=[head_scr, head_scr, head_scr, head_scr,
                        pltpu.VMEM((N_PEER_HEADS, 8, tm), F32), pltpu.VMEM((d, tm), F32)],
        compiler_params=_params("parallel", "arbitrary"),
        name="peer",
    )(hp, x1, wpqt, sk, eu, evt, gf)


def _split_w_in(w_in_l):
    a = 3 * ATTN_WIDTH + N_IDX_HEADS * IDX_DIM
    b = a + IDX_DIM + N_IDX_HEADS
    wm = jnp.concatenate([w_in_l[:, :a], w_in_l[:, b:]], axis=1).astype(BF16)
    ws = jnp.pad(w_in_l[:, a:b], ((0, 0), (0, LANES - (b - a)))).astype(BF16)
    return wm, ws


def _prompt_dsa_operands(q, k, v, qi, small, b, t, tq, tk):
    nq = t // tq
    qi_t = qi.reshape(b, nq, tq, N_IDX_HEADS, IDX_DIM).transpose(0, 1, 4, 3, 2)
    qi_t = qi_t.reshape(b, nq, IDX_DIM, N_IDX_HEADS * tq).astype(BF16)
    wi = small[:, IDX_DIM:IDX_DIM + N_IDX_HEADS]
    w_t = wi.reshape(b, nq, tq, N_IDX_HEADS).transpose(0, 1, 3, 2).reshape(b, nq, 1, N_IDX_HEADS * tq)
    ki = small[:, :IDX_DIM].reshape(b, t, IDX_DIM).astype(BF16)
    qt = q.astype(BF16).reshape(b, nq, tq, N_HEADS // 2, 2, HEAD_DIM).transpose(0, 1, 3, 4, 5, 2)
    eye = jnp.eye(2, dtype=BF16)
    qbd = qt[:, :, :, :, :, None, :] * eye[None, None, None, :, None, :, None]
    qbd = qbd.reshape(b, nq, N_HEADS // 2, 2 * HEAD_DIM, 2 * tq)
    kb = k.astype(BF16).reshape(b, t, ATTN_WIDTH)
    vt = v.astype(BF16).reshape(b, t // tk, tk, ATTN_WIDTH).transpose(0, 1, 3, 2)
    return qi_t, w_t, ki, qbd, kb, vt


def _sample_group_attention(qs, ks, vs, qis, smalls, page_table, cache_k_l, cache_v_l, cache_ki_l, bd, tn):
    past = page_table.shape[1] * PAGE_SIZE
    n_rows = N_HEADS * tn
    row_pad = ((0, 0), (0, LANES - n_rows), (0, 0))
    key_pad = ((0, 0), (0, 0), (0, PAGE_SIZE - tn))
    head_rows = lambda x: x.reshape(bd, tn, N_HEADS, -1).transpose(0, 2, 1, 3).reshape(bd, n_rows, -1)
    qi_rows = jnp.pad(head_rows(qis), row_pad).astype(BF16)
    w_col = jnp.pad(head_rows(smalls[:, IDX_DIM:IDX_DIM + N_IDX_HEADS]), row_pad)
    ki_new_t = jnp.pad(smalls[:, :IDX_DIM].reshape(bd, tn, IDX_DIM).transpose(0, 2, 1), key_pad)
    n_sel = min(MAX_TOPK, (past + tn) // 4)
    keys, keys_new, thr, need = _sample_score(page_table, qi_rows, w_col, ki_new_t,
                                              cache_ki_l.transpose(0, 1, 3, 2), n_sel, tn)
    q_bd = jnp.einsum('bqhd,hg->bhqgd', qs.reshape(bd, tn, N_HEADS, HEAD_DIM), jnp.eye(N_HEADS, dtype=F32))
    q_bd = jnp.pad(q_bd.reshape(bd, n_rows, ATTN_WIDTH), row_pad).astype(BF16)
    k_new_t = jnp.pad(ks.reshape(bd, tn, ATTN_WIDTH).transpose(0, 2, 1), key_pad)
    v_new_t = jnp.pad(vs.reshape(bd, tn, ATTN_WIDTH).transpose(0, 2, 1), key_pad)
    col = jnp.arange(ATTN_WIDTH)
    diag = (jnp.arange(LANES)[:, None] // tn == col[None, :] // HEAD_DIM).astype(F32)
    fold = (col[:, None] % HEAD_DIM == jnp.arange(HEAD_DIM)[None, :]).astype(BF16)
    earlier = (jnp.arange(PAGE_SIZE)[:, None] < jnp.arange(PAGE_SIZE)[None, :]).astype(BF16)
    tied = (jnp.max(need[:, :, LANES // 2:], axis=(1, 2)) > 0.5).astype(I32)
    out = _sample_attend(page_table, tied, q_bd, keys, keys_new, thr, need, k_new_t, v_new_t, diag, fold,
                         earlier, cache_k_l.transpose(0, 1, 3, 4, 2), cache_v_l.transpose(0, 1, 3, 4, 2))
    return out[:, :n_rows].reshape(bd, N_HEADS, tn, HEAD_DIM).transpose(0, 2, 1, 3).reshape(bd * tn, ATTN_WIDTH)


def _sample_gmlp_weights(w_sp_l, bias_p, tn):
    per_chunk = CHUNK // tn
    tril = jnp.tril(jnp.ones((tn, tn), dtype=bool))
    wsp_s = jnp.einsum('ij,gts->gitjs', jnp.eye(per_chunk, dtype=F32),
                       jnp.where(tril[None], w_sp_l[:, :tn, :tn], 0))
    wsp_s = wsp_s.reshape(N_GROUPS, CHUNK, CHUNK).astype(BF16)
    return wsp_s, jnp.tile(bias_p[:tn], (per_chunk, 1))


def kernel(x_prompt, x_sample, cache_k, cache_v, cache_kidx, page_table, norm_mix_g, w_in,
           gv_norm_g, w_sp, b_sp, w_out, norm_ffn_g, w_pq, peer_sub_keys, expert_u, expert_v,
           norm_final_g):
    b, t, d = x_prompt.shape
    bd, tn, _ = x_sample.shape
    depth = w_in.shape[0]
    assert depth == 1 and tn == 8 and t % 256 == 0 and (bd * tn) % CHUNK == 0
    n_pages = page_table.shape[1]
    past = n_pages * PAGE_SIZE
    l = 0

    wm, ws = _split_w_in(w_in[l])
    g_mix = norm_mix_g[l].reshape(1, d)
    gvg = gv_norm_g[l].reshape(1, GMLP_WIDTH)
    lane_group = jnp.arange(GMLP_WIDTH) // HEAD_DIM
    gmean = (lane_group[:, None] == lane_group[None, :]).astype(BF16) * (1.0 / HEAD_DIM)
    invf = ROPE_THETA ** (-jnp.arange(ROPE_HALF, dtype=F32) / ROPE_HALF)
    invf = jnp.tile(invf, LANES // ROPE_HALF).reshape(1, LANES)
    tril = jnp.tril(jnp.ones((CHUNK, CHUNK), dtype=bool))
    wsp_p = jnp.where(tril[None], w_sp[l], 0).astype(BF16)
    bias_p = jnp.repeat(b_sp[l].T, HEAD_DIM, axis=1)
    wsp_s, bias_s = _sample_gmlp_weights(w_sp[l], bias_p, tn)
    wo =w_out[l].astype(BF16)
    g_ffn = norm_ffn_g[l].reshape(1, d)
    wpqt = w_pq[l].T.astype(BF16)
    sk = peer_sub_keys[l].reshape(2 * N_PEER_HEADS, N_KEYS, PEER_HALF).astype(BF16)
    eu = expert_u[l].astype(BF16)
    evt = expert_v[l].T.astype(BF16)
    gf = norm_final_g.reshape(1, d)

    n_p = b * t
    pos_p = jnp.tile(jnp.arange(t, dtype=F32), b).reshape(n_p, 1)
    q, k, v, qi, u, gv, small = _project(x_prompt.reshape(n_p, d), pos_p, invf, g_mix, wm, ws,
                                         gvg, gmean, 256)
    tq, tk = 256, 256
    ops = _prompt_dsa_operands(q, k, v, qi, small, b, t, tq, tk)
    a_t = _dsa_prompt(*ops, tq, tk, min(MAX_TOPK, t // 4))
    a_p = a_t.transpose(0, 2, 1).reshape(n_p, ATTN_WIDTH).astype(BF16)
    x1_p, hp_p = _mix(a_p, u, gv, x_prompt.reshape(n_p, d), wsp_p, bias_p, wo, g_ffn)

    n_s = bd * tn
    pos_s = jnp.tile(past + jnp.arange(tn, dtype=F32), bd).reshape(n_s, 1)
    qs, ks, vs, qis, us, gvs, smalls = _project(x_sample.reshape(n_s, d), pos_s, invf, g_mix, wm,
                                                ws, gvg, gmean, n_s)
    a_s = _sample_group_attention(qs, ks, vs, qis, smalls, page_table, cache_k, cache_v,
                                  cache_kidx, bd, tn).astype(BF16)
    x1_s, hp_s = _mix(a_s, us, gvs, x_sample.reshape(n_s, d), wsp_s, bias_s, wo, g_ffn)

    x1 = jnp.concatenate([x1_p, x1_s], axis=0)
    hp = jnp.concatenate([hp_p, hp_s], axis=0)
    n_all = n_p + n_s
    tm = 640 if n_all % 640 == 0 else 128
    y = _peer(hp, x1, wpqt, sk, eu, evt, gf, tm, 1024)

    heads = (N_HEADS, HEAD_DIM)
    return (y[:n_p].reshape(b, t, d),
            y[n_p:].reshape(bd, tn, d),
            k.reshape(1, b, t, *heads),
            v.reshape(1, b, t, *heads),
            small[:, :IDX_DIM].reshape(1, b, t, IDX_DIM),
            ks.reshape(1, bd, tn, *heads),
            vs.reshape(1, bd, tn, *heads),
            smalls[:, :IDX_DIM].reshape(1, bd, tn, IDX_DIM),
            gvs.reshape(1, bd, tn, N_GROUPS, HEAD_DIM))
```
